```python
import math
import jax, jax.numpy as jnp
from jax import lax
import numpy as np

D_MODEL = 2048
BATCH = 2
SEQ = 8192
DEPTH = 2

HEAD_DIM = 64
N_Q_HEADS = 16
N_KV_HEADS = 4
GQA_GROUP = N_Q_HEADS // N_KV_HEADS
WINDOW = 128
ATTN_BLOCK = 128
ATTN_WIDTH = N_Q_HEADS * HEAD_DIM

GLA_HEADS = 4
GLA_DV = 256
GLA_DK = 128
GLA_WIDTH = GLA_HEADS * GLA_DV
GLA_GATE_RANK = 16
GLA_TAU = 16.0
GLA_CHUNK = 64

MIX_WIDTH = ATTN_WIDTH + GLA_WIDTH
D_FF = 4 * D_MODEL
EPS = 1e-6

IN_SIZES = (
    ATTN_WIDTH,
    N_KV_HEADS * HEAD_DIM,
    N_KV_HEADS * HEAD_DIM,
    GLA_HEADS * GLA_DK,
    GLA_HEADS * GLA_DK,
    GLA_WIDTH,
    GLA_WIDTH,
    GLA_GATE_RANK,
)
IN_WIDTH = int(sum(IN_SIZES))
SPLIT_POINTS = [int(v) for v in np.cumsum(IN_SIZES)[:-1]]

kernel_name = "hybrid_swa_gla_parallel_heads"


def rmsnorm(x, g):
    xf = x.astype(jnp.float32)
    y = xf * lax.rsqrt(jnp.mean(xf * xf, axis=-1, keepdims=True) + EPS)
    return (y * g.astype(jnp.float32)).astype(x.dtype)


def alibi_slopes(n):
    return jnp.asarray([2.0 ** (-8.0 * (i + 1) / n) for i in range(n)], dtype=jnp.float32)


def sliding_window_attention(q, k, v, sinks):
    B, S = q.shape[0], q.shape[1]
    nb = S // ATTN_BLOCK
    qb = q.reshape(B, nb, ATTN_BLOCK, N_KV_HEADS, GQA_GROUP, HEAD_DIM)

    def band(t):
        tb = t.reshape(B, nb, ATTN_BLOCK, N_KV_HEADS, HEAD_DIM)
        prev = jnp.concatenate([jnp.zeros_like(tb[:, :1]), tb[:, :-1]], axis=1)
        return jnp.concatenate([prev, tb], axis=2)

    kb, vb = band(k), band(v)
    scores = jnp.einsum('bnqhgd,bnkhd->bnhgqk', qb, kb).astype(jnp.float32) * (HEAD_DIM ** -0.5)

    qi = jnp.arange(ATTN_BLOCK)[:, None]
    kj = jnp.arange(2 * ATTN_BLOCK)[None, :]
    dist = qi + ATTN_BLOCK - kj
    blk = jnp.arange(nb)[:, None, None]
    valid = (dist >= 0)[None] & (dist < WINDOW)[None] & ((blk * ATTN_BLOCK - ATTN_BLOCK + kj[None]) >= 0)
    slopes = alibi_slopes(N_Q_HEADS).reshape(N_KV_HEADS, GQA_GROUP)
    alibi = -slopes[:, :, None, None] * dist.astype(jnp.float32)[None, None]
    scores = scores + alibi[None, None]
    scores = jnp.where(valid[None, :, None, None], scores, jnp.float32(-1e30))

    sink = sinks.astype(jnp.float32).reshape(N_KV_HEADS, GQA_GROUP)[None, None, :, :, None]
    m = jnp.maximum(jnp.max(scores, axis=-1), sink)
    p = jnp.exp(scores - m[..., None])
    denom = jnp.sum(p, axis=-1) + jnp.exp(sink - m)
    probs = (p / denom[..., None]).astype(vb.dtype)
    out = jnp.einsum('bnhgqk,bnkhd->bnqhgd', probs, vb)
    return out.reshape(B, S, N_Q_HEADS * HEAD_DIM)


def gla_chunked(q, k, v, log_a):
    B, S, H, dk = q.shape
    dv = v.shape[-1]
    C = GLA_CHUNK
    nc = S // C
    q, k, v, log_a = [t.reshape(B, nc, C, H, t.shape[-1]) for t in (q, k, v, log_a)]
    b = jnp.cumsum(log_a, axis=2)
    b_last = b[:, :, -1:]
    q_in = q * jnp.exp(b)
    k_in = k * jnp.exp(-b)
    k_state = k * jnp.exp(b_last - b)

    causal = jnp.tril(jnp.ones((C, C), dtype=bool))
    att = jnp.einsum('bnthd,bnshd->bnhts', q_in, k_in)
    att = jnp.where(causal, att, 0.0)
    o_intra = jnp.einsum('bnhts,bnshv->bnthv', att, v)

    def step(state, xs):
        qc, kc, vc, decay = xs
        o = jnp.einsum('bthd,bhdv->bthv', qc, state)
        state = state * decay[..., None] + jnp.einsum('bthd,bthv->bhdv', kc, vc)
        return state, o

    xs = (jnp.moveaxis(q_in, 1, 0), jnp.moveaxis(k_state, 1, 0), jnp.moveaxis(v, 1, 0),
          jnp.moveaxis(jnp.exp(b_last[:, :, 0]), 1, 0))
    state0 = jnp.zeros((B, H, dk, dv), jnp.float32)
    _, o_inter = lax.scan(step, state0, xs)
    o = o_intra + jnp.moveaxis(o_inter, 0, 1)
    return o.reshape(B, S, H, dv)


def setup_inputs(seed: int = 0) -> dict:
    key = jax.random.key(seed)
    ks = jax.random.split(key, 16)
    f32 = jnp.float32
    nrm = lambda k, shape, s: jax.random.normal(k, shape, f32) * s
    return {
        "x": nrm(ks[0], (BATCH, SEQ, D_MODEL), 1.0),
        "norm1_g": 1.0 + nrm(ks[1], (DEPTH, D_MODEL), 0.02),
        "w_in": nrm(ks[2], (DEPTH, D_MODEL, IN_WIDTH), D_MODEL ** -0.5),
        "q_norm_g": 1.0 + nrm(ks[3], (DEPTH, HEAD_DIM), 0.02),
        "k_norm_g": 1.0 + nrm(ks[4], (DEPTH, HEAD_DIM), 0.02),
        "attn_sinks": nrm(ks[5], (DEPTH, N_Q_HEADS), 0.5),
        "gla_gate_w": nrm(ks[6], (DEPTH, GLA_GATE_RANK, GLA_HEADS * GLA_DK), GLA_GATE_RANK ** -0.5),
        "gla_gate_b": nrm(ks[7], (DEPTH, GLA_HEADS * GLA_DK), 0.1),
        "gla_norm_g": 1.0 + nrm(ks[8], (DEPTH, GLA_DV), 0.02),
        "w_out": nrm(ks[9], (DEPTH, MIX_WIDTH, D_MODEL), MIX_WIDTH ** -0.5),
        "norm2_g": 1.0 + nrm(ks[10], (DEPTH, D_MODEL), 0.02),
        "w_up": nrm(ks[11], (DEPTH, D_MODEL, D_FF), D_MODEL ** -0.5),
        "w_down": nrm(ks[12], (DEPTH, D_FF, D_MODEL), D_FF ** -0.5),
    }


def reference(x, norm1_g, w_in, q_norm_g, k_norm_g, attn_sinks, gla_gate_w, gla_gate_b,
              gla_norm_g, w_out, norm2_g, w_up, w_down):
    B, S, _ = x.shape
    for l in range(DEPTH):
        h = rmsnorm(x, norm1_g[l])
        proj = h @ w_in[l]
        aq, ak, av, gq, gk, gv, gr, gz = jnp.split(proj, SPLIT_POINTS, axis=-1)

        aq = rmsnorm(aq.reshape(B, S, N_Q_HEADS, HEAD_DIM), q_norm_g[l])
        ak = rmsnorm(ak.reshape(B, S, N_KV_HEADS, HEAD_DIM), k_norm_g[l])
        av = av.reshape(B, S, N_KV_HEADS, HEAD_DIM)
        o_attn = sliding_window_attention(aq, ak, av, attn_sinks[l])

        gq = gq.reshape(B, S, GLA_HEADS, GLA_DK).astype(jnp.float32) * (GLA_DK ** -0.5)
        gk = gk.reshape(B, S, GLA_HEADS, GLA_DK).astype(jnp.float32)
        gv = gv.reshape(B, S, GLA_HEADS, GLA_DV).astype(jnp.float32)
        gate_logit = (gz @ gla_gate_w[l] + gla_gate_b[l]).astype(jnp.float32)
        log_a = (jax.nn.log_sigmoid(gate_logit) / GLA_TAU).reshape(B, S, GLA_HEADS, GLA_DK)
        o_gla = gla_chunked(gq, gk, gv, log_a).astype(x.dtype)
        o_gla = rmsnorm(o_gla, gla_norm_g[l]).reshape(B, S, GLA_WIDTH)
        o_gla = o_gla * jax.nn.silu(gr)

        mix = jnp.concatenate([o_attn, o_gla], axis=-1)
        x = x + mix @ w_out[l]

        h = rmsnorm(x, norm2_g[l])
        u = jnp.square(jax.nn.relu(h @ w_up[l]))
        x = x + u @ w_down[l]
    return x
```

```python
import functools

import jax
import jax.numpy as jnp
from jax import lax
from jax.experimental import pallas as pl
from jax.experimental.pallas import tpu as pltpu

F32 = jnp.float32
BF16 = jnp.bfloat16

D_MODEL = 2048
HEAD_DIM = 64
N_Q_HEADS = 16
N_KV_HEADS = 4
ATTN_BLOCK = 128
ATTN_WIDTH = N_Q_HEADS * HEAD_DIM
KV_WIDTH = N_KV_HEADS * HEAD_DIM
GLA_HEADS = 4
GLA_DK = 128
GLA_DV = 256
GLA_WIDTH = GLA_HEADS * GLA_DV
GLA_RANK = 16
GLA_TAU = 16.0
GLA_CHUNK = 64
D_FF = 4 * D_MODEL
EPS = 1e-6
NEG = -1e30

LANES = 128
MAIN_WIDTH = 4608
OFF_AQ, OFF_AK, OFF_AV = 0, 1024, 1280
OFF_GQ, OFF_GK, OFF_GV, OFF_GR = 1536, 2048, 2560, 3584

VMEM_LIMIT = 56 * 1024 * 1024

ALIBI_SLOPES = tuple(2.0 ** (-8.0 * (i + 1) / N_Q_HEADS) for i in range(N_Q_HEADS))


def _params(n_axes):
    return pltpu.CompilerParams(dimension_semantics=("arbitrary",) * n_axes,
                                vmem_limit_bytes=VMEM_LIMIT)


def _rmsnorm_rows(x, g):
    ms = jnp.mean(x * x, axis=-1, keepdims=True)
    return (x * lax.rsqrt(ms + EPS)) * g


def _in_proj_kernel(x_ref, g_ref, w_ref, wz_ref, o_ref, z_ref, hn_ref, *, row_chunk):
    j = pl.program_id(1)

    @pl.when(j == 0)
    def _():
        def body(r, carry):
            r0 = pl.multiple_of(r * row_chunk, row_chunk)
            hn = _rmsnorm_rows(x_ref[pl.ds(r0, row_chunk), :], g_ref[...])
            hn_ref[pl.ds(r0, row_chunk), :] = hn.astype(BF16)
            return carry
        lax.fori_loop(0, x_ref.shape[0] // row_chunk, body, 0)
        z_ref[...] = jnp.dot(hn_ref[...], wz_ref[...], preferred_element_type=F32)

    o_ref[...] = jnp.dot(hn_ref[...], w_ref[...], preferred_element_type=F32).astype(o_ref.dtype)


def _in_proj(x2, g, w_main, w_z, tm, tn):
    T = x2.shape[0]
    grid = (T // tm, MAIN_WIDTH // tn)
    return pl.pallas_call(
        functools.partial(_in_proj_kernel, row_chunk=min(128, tm)),
        grid=grid,
        in_specs=[
            pl.BlockSpec((tm, D_MODEL), lambda i, j: (i, 0)),
            pl.BlockSpec((1, D_MODEL), lambda i, j: (0, 0)),
            pl.BlockSpec((D_MODEL, tn), lambda i, j: (0, j)),
            pl.BlockSpec((D_MODEL, LANES), lambda i, j: (0, 0)),
        ],
        out_specs=[
            pl.BlockSpec((tm, tn), lambda i, j: (i, j)),
            pl.BlockSpec((tm, LANES), lambda i, j: (i, 0)),
        ],
        out_shape=[
            jax.ShapeDtypeStruct((T, MAIN_WIDTH), BF16),
            jax.ShapeDtypeStruct((T, LANES), F32),
        ],
        scratch_shapes=[pltpu.VMEM((tm, D_MODEL), BF16)],
        compiler_params=_params(2),
        name="in_proj",
    )(x2, g, w_main, w_z)


def _head_sumsq(xt):
    r = lax.broadcasted_iota(jnp.int32, (LANES, LANES), 0) // HEAD_DIM
    c = lax.broadcasted_iota(jnp.int32, (LANES, LANES), 1) // HEAD_DIM
    ones_blockdiag = (r == c).astype(BF16)
    return jnp.dot((xt * xt).astype(BF16), ones_blockdiag, preferred_element_type=F32)


def _head_rmsnorm(xt, g):
    return xt * lax.rsqrt(_head_sumsq(xt) * (1.0 / HEAD_DIM) + EPS) * g


def _swa_kernel(sink_ref, q_ref, k_ref, v_ref, gq_ref, gk_ref, o_ref,
                qn_ref, kk_ref, vv_ref, tbl_ref, *, rows):
    b = pl.program_id(0)
    n = pl.program_id(1)
    blk = ATTN_BLOCK
    low = lax.broadcasted_iota(jnp.int32, (1, LANES), 1) < HEAD_DIM

    @pl.when(jnp.logical_and(b == 0, n == 0))
    def _():
        qi = lax.broadcasted_iota(jnp.int32, (blk, 2 * blk), 0)
        kj = lax.broadcasted_iota(jnp.int32, (blk, 2 * blk), 1)
        dist = qi + blk - kj
        valid = jnp.logical_and(dist >= 0, dist < blk)
        distf = dist.astype(F32)
        for qh in range(N_Q_HEADS):
            tbl_ref[qh] = jnp.where(valid, -ALIBI_SLOPES[qh] * distf, NEG)

    @pl.when(n == 0)
    def _():
        kk_ref[:, 0:blk, :] = jnp.zeros((2 * N_KV_HEADS, blk, LANES), BF16)
        vv_ref[:, 0:blk, :] = jnp.zeros((2 * N_KV_HEADS, blk, LANES), BF16)

    for t in range(ATTN_WIDTH // LANES):
        qt = q_ref[:, t * LANES:(t + 1) * LANES].astype(F32)
        qn = _head_rmsnorm(qt, gq_ref[...]) * (HEAD_DIM ** -0.5)
        qn_ref[:, t * LANES:(t + 1) * LANES] = qn.astype(BF16)

    for p in range(KV_WIDTH // LANES):
        kt = _head_rmsnorm(k_ref[:, p * LANES:(p + 1) * LANES].astype(F32), gk_ref[...])
        vt = v_ref[:, p * LANES:(p + 1) * LANES].astype(F32)
        k_even = jnp.where(low, kt, 0.0)
        k_odd = jnp.where(low, 0.0, kt)
        v_even = jnp.where(low, vt, 1.0)
        v_odd = jnp.where(low, 1.0, vt)
        h0, h1 = 2 * p, 2 * p + 1
        kk_ref[2 * h0 + 0, blk:, :] = k_even.astype(BF16)
        kk_ref[2 * h0 + 1, blk:, :] = pltpu.roll(k_even, HEAD_DIM, 1).astype(BF16)
        kk_ref[2 * h1 + 1, blk:, :] = k_odd.astype(BF16)
        kk_ref[2 * h1 + 0, blk:, :] = pltpu.roll(k_odd, HEAD_DIM, 1).astype(BF16)
        vv_ref[2 * h0 + 0, blk:, :] = v_even.astype(BF16)
        vv_ref[2 * h0 + 1, blk:, :] = pltpu.roll(v_even, HEAD_DIM, 1).astype(BF16)
        vv_ref[2 * h1 + 1, blk:, :] = v_odd.astype(BF16)
        vv_ref[2 * h1 + 0, blk:, :] = pltpu.roll(v_odd, HEAD_DIM, 1).astype(BF16)

    prev_cols = lax.broadcasted_iota(jnp.int32, (1, 2 * blk), 1) < blk

    def block_body(nb, carry):
        r0 = pl.multiple_of(nb * blk, blk)
        seq_start = jnp.logical_and(n == 0, nb == 0)
        negrow = jnp.where(jnp.logical_and(seq_start, prev_cols), NEG, 0.0).astype(F32)
        for h in range(N_KV_HEADS):
            for tt in range(2):
                t = 2 * h + tt
                qs = qn_ref[pl.ds(r0, blk), t * LANES:(t + 1) * LANES]
                res, sink_terms = [], []
                for a in range(2):
                    qh = 4 * h + 2 * tt + a
                    keys = kk_ref[2 * h + a, pl.ds(r0, 2 * blk), :]
                    s = lax.dot_general(qs, keys, (((1,), (1,)), ((), ())),
                                        preferred_element_type=F32)
                    s = s + tbl_ref[qh] + negrow
                    sink = sink_ref[qh]
                    m = jnp.maximum(jnp.max(s, axis=-1, keepdims=True), sink)
                    p = jnp.exp(s - m).astype(BF16)
                    vals = vv_ref[2 * h + a, pl.ds(r0, 2 * blk), :]
                    res.append(jnp.dot(p, vals, preferred_element_type=F32))
                    sink_terms.append(jnp.exp(sink - m))
                num = jnp.where(low, res[0], res[1])
                den = pltpu.roll(jnp.where(low, res[1], res[0]), HEAD_DIM, 1)
                den = den + jnp.where(low, sink_terms[0], sink_terms[1])
                o_ref[pl.ds(r0, blk), t * LANES:(t + 1) * LANES] = (num / den).astype(o_ref.dtype)
        return carry

    lax.fori_loop(0, rows // blk, block_body, 0)

    kk_ref[:, 0:blk, :] = kk_ref[:, rows:rows + blk, :]
    vv_ref[:, 0:blk, :] = vv_ref[:, rows:rows + blk, :]


def _swa(proj, sinks, gq, gk, B, S, rows):
    T = B * S
    nsteps = S // rows
    qcol = OFF_AQ // ATTN_WIDTH
    kcol = OFF_AK // KV_WIDTH
    vcol = OFF_AV // KV_WIDTH
    return pl.pallas_call(
        functools.partial(_swa_kernel, rows=rows),
        grid=(B, nsteps),
        in_specs=[
            pl.BlockSpec(memory_space=pltpu.SMEM),
            pl.BlockSpec((rows, ATTN_WIDTH), lambda b, n: (b * nsteps + n, qcol)),
            pl.BlockSpec((rows, KV_WIDTH), lambda b, n: (b * nsteps + n, kcol)),
            pl.BlockSpec((rows, KV_WIDTH), lambda b, n: (b * nsteps + n, vcol)),
            pl.BlockSpec((1, LANES), lambda b, n: (0, 0)),
            pl.BlockSpec((1, LANES), lambda b, n: (0, 0)),
        ],
        out_specs=pl.BlockSpec((rows, ATTN_WIDTH), lambda b, n: (b * nsteps + n, 0)),
        out_shape=jax.ShapeDtypeStruct((T, ATTN_WIDTH), BF16),
        scratch_shapes=[
            pltpu.VMEM((rows, ATTN_WIDTH), BF16),
            pltpu.VMEM((2 * N_KV_HEADS, rows + ATTN_BLOCK, LANES), BF16),
            pltpu.VMEM((2 * N_KV_HEADS, rows + ATTN_BLOCK, LANES), BF16),
            pltpu.VMEM((N_Q_HEADS, ATTN_BLOCK, 2 * ATTN_BLOCK), F32),
        ],
        compiler_params=_params(2),
        name="swa",
    )(sinks, proj, proj, proj, gq, gk)


def _gla_kernel(q_ref, k_ref, v_ref, r_ref, z_ref, gw_ref, gb_ref, ng_ref, o_ref, s_ref, *, rows):
    c = pl.program_id(2)
    pair = 2 * GLA_CHUNK

    @pl.when(c == 0)
    def _():
        s_ref[...] = jnp.zeros_like(s_ref)

    ri = lax.broadcasted_iota(jnp.int32, (pair, pair), 0)
    ci = lax.broadcasted_iota(jnp.int32, (pair, pair), 1)
    causal = jnp.logical_and(ri // GLA_CHUNK == ci // GLA_CHUNK, ci <= ri)
    tril = causal.astype(BF16)
    low = lax.broadcasted_iota(jnp.int32, (1, pair), 1) < GLA_CHUNK

    logit = jnp.dot(z_ref[...].astype(BF16), gw_ref[...], preferred_element_type=F32) + gb_ref[...]
    log_a = jax.nn.log_sigmoid(logit) * (1.0 / GLA_TAU)

    state = s_ref[...]
    for pr in range(rows // pair):
        sl = slice(pr * pair, (pr + 1) * pair)
        la = log_a[sl]
        la_hi = la.astype(BF16)
        la_lo = (la - la_hi.astype(F32)).astype(BF16)
        bcum = (jnp.dot(tril, la_hi, preferred_element_type=F32)
                + jnp.dot(tril, la_lo, preferred_element_type=F32))
        q = q_ref[sl, :].astype(F32) * (GLA_DK ** -0.5)
        k = k_ref[sl, :].astype(F32)
        v = v_ref[sl, :]
        q_in = (q * jnp.exp(bcum)).astype(BF16)
        k_in = (k * jnp.exp(-bcum)).astype(BF16)
        att = lax.dot_general(q_in, k_in, (((1,), (1,)), ((), ())), preferred_element_type=F32)
        att = jnp.where(causal, att, 0.0).astype(BF16)
        o_intra = jnp.dot(att, v, preferred_element_type=F32)

        bt = bcum.T
        kt = k.T
        last_a = bt[:, GLA_CHUNK - 1:GLA_CHUNK]
        last_b = bt[:, pair - 1:pair]
        kst = kt * jnp.exp(jnp.where(low, last_a, last_b) - bt)
        kst_a = jnp.where(low, kst, 0.0).astype(BF16)
        kst_b = jnp.where(low, 0.0, kst).astype(BF16)

        o_a = jnp.dot(q_in[0:GLA_CHUNK], state.astype(BF16), preferred_element_type=F32)
        state = state * jnp.exp(last_a) + jnp.dot(kst_a, v, preferred_element_type=F32)
        o_b = jnp.dot(q_in[GLA_CHUNK:pair], state.astype(BF16), preferred_element_type=F32)
        state = state * jnp.exp(last_b) + jnp.dot(kst_b, v, preferred_element_type=F32)

        o = o_intra + jnp.concatenate([o_a, o_b], axis=0)
        y = _rmsnorm_rows(o, ng_ref[...])
        gr = r_ref[sl, :].astype(F32)
        o_ref[sl, :] = (y * (gr * jax.nn.sigmoid(gr))).astype(o_ref.dtype)
    s_ref[...] = state


def _gla(proj, z, gate_w, gate_b, norm_g, B, S, rows):
    T = B * S
    nsteps = S // rows
    cq = OFF_GQ // GLA_DK
    ck = OFF_GK // GLA_DK
    cv = OFF_GV // GLA_DV
    cr = OFF_GR // GLA_DV
    row = lambda b, h, c: b * nsteps + c
    return pl.pallas_call(
        functools.partial(_gla_kernel, rows=rows),
        grid=(B, GLA_HEADS, nsteps),
        in_specs=[
            pl.BlockSpec((rows, GLA_DK), lambda b, h, c: (row(b, h, c), cq + h)),
            pl.BlockSpec((rows, GLA_DK), lambda b, h, c: (row(b, h, c), ck + h)),
            pl.BlockSpec((rows, GLA_DV), lambda b, h, c: (row(b, h, c), cv + h)),
            pl.BlockSpec((rows, GLA_DV), lambda b, h, c: (row(b, h, c), cr + h)),
            pl.BlockSpec((rows, LANES), lambda b, h, c: (row(b, h, c), 0)),
            pl.BlockSpec((LANES, GLA_DK), lambda b, h, c: (0, h)),
            pl.BlockSpec((1, GLA_DK), lambda b, h, c: (0, h)),
            pl.BlockSpec((1, GLA_DV), lambda b, h, c: (0, 0)),
        ],
        out_specs=pl.BlockSpec((rows, GLA_DV), lambda b, h, c: (row(b, h, c), h)),
        out_shape=jax.ShapeDtypeStruct((T, GLA_WIDTH), BF16),
        scratch_shapes=[pltpu.VMEM((GLA_DK, GLA_DV), F32)],
        compiler_params=_params(3),
        name="gla",
    )(proj, proj, proj, proj, z, gate_w, gate_b, norm_g)


def _out_proj_kernel(x_ref, a_ref, g_ref, wa_ref, wg_ref, o_ref):
    acc = jnp.dot(a_ref[...], wa_ref[...], preferred_element_type=F32)
    acc = acc + jnp.dot(g_ref[...], wg_ref[...], preferred_element_type=F32)
    o_ref[...] = x_ref[...] + acc


def _out_proj(x2, o_attn, o_gla, w_out, tm, tn):
    T = x2.shape[0]
    return pl.pallas_call(
        _out_proj_kernel,
        grid=(T // tm, D_MODEL // tn),
        in_specs=[
            pl.BlockSpec((tm, tn), lambda i, j: (i, j)),
            pl.BlockSpec((tm, ATTN_WIDTH), lambda i, j: (i, 0)),
            pl.BlockSpec((tm, GLA_WIDTH), lambda i, j: (i, 0)),
            pl.BlockSpec((ATTN_WIDTH, tn), lambda i, j: (0, j)),
            pl.BlockSpec((GLA_WIDTH, tn), lambda i, j: (1, j)),
        ],
        out_specs=pl.BlockSpec((tm, tn), lambda i, j: (i, j)),
        out_shape=jax.ShapeDtypeStruct((T, D_MODEL), F32),
        compiler_params=_params(2),
        name="out_proj",
    )(x2, o_attn, o_gla, w_out, w_out)


def _ffn_kernel(x_ref, g_ref, wu_ref, wd_ref, o_ref, hn_ref, *, row_chunk):
    f = pl.program_id(1)

    @pl.when(f == 0)
    def _():
        def body(r, carry):
            r0 = pl.multiple_of(r * row_chunk, row_chunk)
            xr = x_ref[pl.ds(r0, row_chunk), :]
            hn_ref[pl.ds(r0, row_chunk), :] = _rmsnorm_rows(xr, g_ref[...]).astype(BF16)
            o_ref[pl.ds(r0, row_chunk), :] = xr
            return carry
        lax.fori_loop(0, x_ref.shape[0] // row_chunk, body, 0)

    u = jnp.dot(hn_ref[...], wu_ref[...], preferred_element_type=F32)
    u = jnp.maximum(u, 0.0)
    u = (u * u).astype(BF16)
    o_ref[...] += jnp.dot(u, wd_ref[...], preferred_element_type=F32)


def _ffn(x2, g, w_up, w_down, tm, tf):
    T = x2.shape[0]
    return pl.pallas_call(
        functools.partial(_ffn_kernel, row_chunk=min(128, tm)),
        grid=(T // tm, D_FF // tf),
        in_specs=[
            pl.BlockSpec((tm, D_MODEL), lambda i, f: (i, 0)),
            pl.BlockSpec((1, D_MODEL), lambda i, f: (0, 0)),
            pl.BlockSpec((D_MODEL, tf), lambda i, f: (0, f)),
            pl.BlockSpec((tf, D_MODEL), lambda i, f: (f, 0)),
        ],
        out_specs=pl.BlockSpec((tm, D_MODEL), lambda i, f: (i, 0)),
        out_shape=jax.ShapeDtypeStruct((T, D_MODEL), F32),
        scratch_shapes=[pltpu.VMEM((tm, D_MODEL), BF16)],
        compiler_params=_params(2),
        name="ffn",
    )(x2, g, w_up, w_down)


def _tile_plan(T, S):
    tm = min(1024, T)
    mix_rows = min(512, S)
    assert T % tm == 0 and S % mix_rows == 0 and mix_rows % (2 * GLA_CHUNK) == 0
    return dict(tm=tm, tn_in=1536, tn_out=1024, tf=512, mix_rows=mix_rows)


def kernel(x, norm1_g, w_in, q_norm_g, k_norm_g, attn_sinks, gla_gate_w, gla_gate_b, gla_norm_g,
           w_out, norm2_g, w_up, w_down):
    B, S, D = x.shape
    assert D == D_MODEL
    T = B * S
    plan = _tile_plan(T, S)
    depth = w_in.shape[0]
    x2 = x.reshape(T, D)

    for l in range(depth):
        w_main = w_in[l, :, :MAIN_WIDTH].astype(BF16)
        w_z = jnp.pad(w_in[l, :, MAIN_WIDTH:], ((0, 0), (0, LANES - GLA_RANK))).astype(BF16)
        gate_w = jnp.pad(gla_gate_w[l], ((0, LANES - GLA_RANK), (0, 0))).astype(BF16)
        gq = jnp.tile(q_norm_g[l], 2).reshape(1, LANES)
        gk = jnp.tile(k_norm_g[l], 2).reshape(1, LANES)

        proj, z = _in_proj(x2, norm1_g[l].reshape(1, D), w_main, w_z, plan["tm"], plan["tn_in"])
        o_attn = _swa(proj, attn_sinks[l], gq, gk, B, S, plan["mix_rows"])
        o_gla = _gla(proj, z, gate_w, gla_gate_b[l].reshape(1, -1), gla_norm_g[l].reshape(1, -1),
                     B, S, plan["mix_rows"])
        x2 = _out_proj(x2, o_attn, o_gla, w_out[l].astype(BF16), plan["tm"], plan["tn_out"])
        x2 = _ffn(x2, norm2_g[l].reshape(1, D), w_up[l].astype(BF16), w_down[l].astype(BF16),
                  plan["tm"], plan["tf"])
    return x2.reshape(B, S, D)
```

```python
import functools

import jax
import jax.numpy as jnp
from jax import lax
from jax.experimental import pallas as pl
from jax.experimental.pallas import tpu as pltpu

F32 = jnp.float32
BF16 = jnp.bfloat16

D_MODEL = 2048
HEAD_DIM = 64
N_Q_HEADS = 16
N_KV_HEADS = 4
ATTN_BLOCK = 128
ATTN_WIDTH = N_Q_HEADS * HEAD_DIM
KV_WIDTH = N_KV_HEADS * HEAD_DIM
GLA_HEADS = 4
GLA_DK = 128
GLA_DV = 256
GLA_QK_WIDTH = GLA_HEADS * GLA_DK
GLA_WIDTH = GLA_HEADS * GLA_DV
GLA_RANK = 16
GLA_TAU = 16.0
GLA_CHUNK = 64
D_FF = 4 * D_MODEL
EPS = 1e-6
NEG = -1e30
LOG2E = 1.4426950408889634

LANES = 128
MAIN_WIDTH = 4608
OFF_AQ, OFF_AK, OFF_AV = 0, 1024, 1280
OFF_GQ, OFF_GK, OFF_GV, OFF_GR = 1536, 2048, 2560, 3584

VMEM_LIMIT = 60 * 1024 * 1024

ALIBI_SLOPES = tuple(2.0 ** (-8.0 * (i + 1) / N_Q_HEADS) for i in range(N_Q_HEADS))


def _params(n_axes):
    return pltpu.CompilerParams(dimension_semantics=("arbitrary",) * n_axes,
                                vmem_limit_bytes=VMEM_LIMIT)


def _rmsnorm_rows(x, g):
    ms = jnp.mean(x * x, axis=-1, keepdims=True)
    return (x * lax.rsqrt(ms + EPS)) * g


def _in_proj_kernel(x_ref, g_ref, w_ref, wz_ref, o_ref, z_ref, hn_ref, *, row_chunk):
    j = pl.program_id(1)

    @pl.when(j == 0)
    def _():
        def body(r, carry):
            r0 = pl.multiple_of(r * row_chunk, row_chunk)
            hn = _rmsnorm_rows(x_ref[pl.ds(r0, row_chunk), :], g_ref[...])
            hn_ref[pl.ds(r0, row_chunk), :] = hn.astype(BF16)
            return carry
        lax.fori_loop(0, x_ref.shape[0] // row_chunk, body, 0)
        z_ref[...] = jnp.dot(hn_ref[...], wz_ref[...], preferred_element_type=F32)

    o_ref[...] = jnp.dot(hn_ref[...], w_ref[...], preferred_element_type=F32).astype(o_ref.dtype)


def _in_proj(x2, g, w_in_b, w_z, l, tm, tn):
    T = x2.shape[0]
    grid = (T // tm, MAIN_WIDTH // tn)
    return pl.pallas_call(
        functools.partial(_in_proj_kernel, row_chunk=min(128, tm)),
        grid=grid,
        in_specs=[
            pl.BlockSpec((tm, D_MODEL), lambda i, j: (i, 0)),
            pl.BlockSpec((None, 1, D_MODEL), lambda i, j: (l, 0, 0)),
            pl.BlockSpec((None, D_MODEL, tn), lambda i, j: (l, 0, j)),
            pl.BlockSpec((None, D_MODEL, LANES), lambda i, j: (l, 0, 0)),
        ],
        out_specs=[
            pl.BlockSpec((tm, tn), lambda i, j: (i, j)),
            pl.BlockSpec((tm, LANES), lambda i, j: (i, 0)),
        ],
        out_shape=[
            jax.ShapeDtypeStruct((T, MAIN_WIDTH), BF16),
            jax.ShapeDtypeStruct((T, LANES), F32),
        ],
        scratch_shapes=[pltpu.VMEM((tm, D_MODEL), BF16)],
        compiler_params=_params(2),
        name="in_proj",
    )(x2, g, w_in_b, w_z)


def _head_sumsq(xt):
    r = lax.broadcasted_iota(jnp.int32, (LANES, LANES), 0) // HEAD_DIM
    c = lax.broadcasted_iota(jnp.int32, (LANES, LANES), 1) // HEAD_DIM
    ones_blockdiag = (r == c).astype(BF16)
    return jnp.dot((xt * xt).astype(BF16), ones_blockdiag, preferred_element_type=F32)


def _head_rmsnorm(xt, g):
    return xt * lax.rsqrt(_head_sumsq(xt) * (1.0 / HEAD_DIM) + EPS) * g


def _swa_kernel(sink_ref, q_ref, k_ref, v_ref, gq_ref, gk_ref, o_ref,
                qn_ref, kk_ref, vv_ref, tbl_ref, *, rows):
    b = pl.program_id(0)
    n = pl.program_id(1)
    blk = ATTN_BLOCK
    low = lax.broadcasted_iota(jnp.int32, (1, LANES), 1) < HEAD_DIM

    @pl.when(jnp.logical_and(b == 0, n == 0))
    def _():
        qi = lax.broadcasted_iota(jnp.int32, (blk, 2 * blk), 0)
        kj = lax.broadcasted_iota(jnp.int32, (blk, 2 * blk), 1)
        dist = qi + blk - kj
        valid = jnp.logical_and(dist >= 0, dist < blk)
        distf = dist.astype(F32)
        for qh in range(N_Q_HEADS):
            tbl_ref[qh] = jnp.where(valid, (-ALIBI_SLOPES[qh] * LOG2E) * distf, NEG)

    @pl.when(n == 0)
    def _():
        kk_ref[:, 0:blk, :] = jnp.zeros((2 * N_KV_HEADS, blk, LANES), BF16)
        vv_ref[:, 0:blk, :] = jnp.zeros((2 * N_KV_HEADS, blk, LANES), BF16)

    for t in range(ATTN_WIDTH // LANES):
        qt = q_ref[:, t * LANES:(t + 1) * LANES].astype(F32)
        qn = _head_rmsnorm(qt, gq_ref[...]) * (HEAD_DIM ** -0.5 * LOG2E)
        qn_ref[:, t * LANES:(t + 1) * LANES] = qn.astype(BF16)

    for p in range(KV_WIDTH // LANES):
        kt = _head_rmsnorm(k_ref[:, p * LANES:(p + 1) * LANES].astype(F32), gk_ref[...])
        vt = v_ref[:, p * LANES:(p + 1) * LANES].astype(F32)
        k_even = jnp.where(low, kt, 0.0)
        k_odd = jnp.where(low, 0.0, kt)
        v_even = jnp.where(low, vt, 1.0)
        v_odd = jnp.where(low, 1.0, vt)
        h0, h1 = 2 * p, 2 * p + 1
        kk_ref[2 * h0 + 0, blk:, :] = k_even.astype(BF16)
        kk_ref[2 * h0 + 1, blk:, :] = pltpu.roll(k_even, HEAD_DIM, 1).astype(BF16)
        kk_ref[2 * h1 + 1, blk:, :] = k_odd.astype(BF16)
        kk_ref[2 * h1 + 0, blk:, :] = pltpu.roll(k_odd, HEAD_DIM, 1).astype(BF16)
        vv_ref[2 * h0 + 0, blk:, :] = v_even.astype(BF16)
        vv_ref[2 * h0 + 1, blk:, :] = pltpu.roll(v_even, HEAD_DIM, 1).astype(BF16)
        vv_ref[2 * h1 + 1, blk:, :] = v_odd.astype(BF16)
        vv_ref[2 * h1 + 0, blk:, :] = pltpu.roll(v_odd, HEAD_DIM, 1).astype(BF16)

    prev_cols = lax.broadcasted_iota(jnp.int32, (1, 2 * blk), 1) < blk

    def block_body(nb, carry):
        r0 = pl.multiple_of(nb * blk, blk)
        seq_start = jnp.logical_and(n == 0, nb == 0)
        negrow = jnp.where(jnp.logical_and(seq_start, prev_cols), NEG, 0.0).astype(F32)
        for h in range(N_KV_HEADS):
            for tt in range(2):
                t = 2 * h + tt
                qs = qn_ref[pl.ds(r0, blk), t * LANES:(t + 1) * LANES]
                res, sink_terms = [], []
                for a in range(2):
                    qh = 4 * h + 2 * tt + a
                    keys = kk_ref[2 * h + a, pl.ds(r0, 2 * blk), :]
                    s = lax.dot_general(qs, keys, (((1,), (1,)), ((), ())),
                                        preferred_element_type=F32)
                    s = s + tbl_ref[qh] + negrow
                    sink = sink_ref[qh] * LOG2E
                    m = jnp.maximum(jnp.max(s, axis=-1, keepdims=True), sink)
                    p = jnp.exp2(s - m).astype(BF16)
                    vals = vv_ref[2 * h + a, pl.ds(r0, 2 * blk), :]
                    res.append(jnp.dot(p, vals, preferred_element_type=F32))
                    sink_terms.append(jnp.exp2(sink - m))
                num = jnp.where(low, res[0], res[1])
                den = pltpu.roll(jnp.where(low, res[1], res[0]), HEAD_DIM, 1)
                den = den + jnp.where(low, sink_terms[0], sink_terms[1])
                o_ref[pl.ds(r0, blk), t * LANES:(t + 1) * LANES] = (num / den).astype(o_ref.dtype)
        return carry

    lax.fori_loop(0, rows // blk, block_body, 0)

    kk_ref[:, 0:blk, :] = kk_ref[:, rows:rows + blk, :]
    vv_ref[:, 0:blk, :] = vv_ref[:, rows:rows + blk, :]


def _swa(proj, sinks, gq, gk, B, S, rows):
    T = B * S
    nsteps = S // rows
    qcol = OFF_AQ // ATTN_WIDTH
    kcol = OFF_AK // KV_WIDTH
    vcol = OFF_AV // KV_WIDTH
    return pl.pallas_call(
        functools.partial(_swa_kernel, rows=rows),
        grid=(B, nsteps),
        in_specs=[
            pl.BlockSpec(memory_space=pltpu.SMEM),
            pl.BlockSpec((rows, ATTN_WIDTH), lambda b, n: (b * nsteps + n, qcol)),
            pl.BlockSpec((rows, KV_WIDTH), lambda b, n: (b * nsteps + n, kcol)),
            pl.BlockSpec((rows, KV_WIDTH), lambda b, n: (b * nsteps + n, vcol)),
            pl.BlockSpec((1, LANES), lambda b, n: (0, 0)),
            pl.BlockSpec((1, LANES), lambda b, n: (0, 0)),
        ],
        out_specs=pl.BlockSpec((rows, ATTN_WIDTH), lambda b, n: (b * nsteps + n, 0)),
        out_shape=jax.ShapeDtypeStruct((T, ATTN_WIDTH), BF16),
        scratch_shapes=[
            pltpu.VMEM((rows, ATTN_WIDTH), BF16),
            pltpu.VMEM((2 * N_KV_HEADS, rows + ATTN_BLOCK, LANES), BF16),
            pltpu.VMEM((2 * N_KV_HEADS, rows + ATTN_BLOCK, LANES), BF16),
            pltpu.VMEM((N_Q_HEADS, ATTN_BLOCK, 2 * ATTN_BLOCK), F32),
        ],
        compiler_params=_params(2),
        name="swa",
    )(sinks, proj, proj, proj, gq, gk)


def _log_sigmoid(x):
    return jnp.minimum(x, 0.0) - jnp.log(1.0 + jnp.exp(jnp.minimum(x, -x)))


def _silu(x):
    h = 0.5 * x
    return h * (1.0 + jnp.tanh(h))


def _gla_kernel(q_ref, k_ref, v_ref, r_ref, z_ref, gw_ref, gb_ref, ng_ref, o_ref,
                s_ref, la_ref, *, rows):
    c = pl.program_id(1)
    pair = 2 * GLA_CHUNK

    @pl.when(c == 0)
    def _():
        s_ref[...] = jnp.zeros_like(s_ref)

    ri = lax.broadcasted_iota(jnp.int32, (pair, pair), 0)
    ci = lax.broadcasted_iota(jnp.int32, (pair, pair), 1)
    causal = jnp.logical_and(ri // GLA_CHUNK == ci // GLA_CHUNK, ci <= ri)
    tril = causal.astype(BF16)
    low = lax.broadcasted_iota(jnp.int32, (1, pair), 1) < GLA_CHUNK

    logit = jnp.dot(z_ref[...].astype(BF16), gw_ref[...], preferred_element_type=F32) + gb_ref[...]
    la_ref[...] = _log_sigmoid(logit) * (1.0 / GLA_TAU)

    def pair_body(pr, carry):
        r0 = pl.multiple_of(pr * pair, pair)
        la = la_ref[pl.ds(r0, pair), :]
        la_hi = la.astype(BF16)
        la_lo = (la - la_hi.astype(F32)).astype(BF16)
        bcum = (jnp.dot(tril, la_hi, preferred_element_type=F32)
                + jnp.dot(tril, la_lo, preferred_element_type=F32))
        q = q_ref[pl.ds(r0, pair), :].astype(F32) * (GLA_DK ** -0.5)
        k = k_ref[pl.ds(r0, pair), :].astype(F32)
        q_in = (q * jnp.exp(bcum)).astype(BF16)
        k_in = (k * jnp.exp(-bcum)).astype(BF16)

        for h in range(GLA_HEADS):
            kc = slice(h * GLA_DK, (h + 1) * GLA_DK)
            vc = slice(h * GLA_DV, (h + 1) * GLA_DV)
            v = v_ref[pl.ds(r0, pair), vc]
            qh = q_in[:, kc]
            att = lax.dot_general(qh, k_in[:, kc], (((1,), (1,)), ((), ())),
                                  preferred_element_type=F32)
            att = jnp.where(causal, att, 0.0).astype(BF16)
            o_intra = jnp.dot(att, v, preferred_element_type=F32)

            bt = bcum[:, kc].T
            kt = k[:, kc].T
            last_a = bt[:, GLA_CHUNK - 1:GLA_CHUNK]
            last_b = bt[:, pair - 1:pair]
            kst = kt * jnp.exp(jnp.where(low, last_a, last_b) - bt)
            kst_a = jnp.where(low, kst, 0.0).astype(BF16)
            kst_b = jnp.where(low, 0.0, kst).astype(BF16)

            state = s_ref[h]
            o_a = jnp.dot(qh[0:GLA_CHUNK], state.astype(BF16), preferred_element_type=F32)
            state = state * jnp.exp(last_a) + jnp.dot(kst_a, v, preferred_element_type=F32)
            o_b = jnp.dot(qh[GLA_CHUNK:pair], state.astype(BF16), preferred_element_type=F32)
            state = state * jnp.exp(last_b) + jnp.dot(kst_b, v, preferred_element_type=F32)
            s_ref[h] = state

            o = o_intra + jnp.concatenate([o_a, o_b], axis=0)
            y = _rmsnorm_rows(o, ng_ref[...])
            gate = _silu(r_ref[pl.ds(r0, pair), vc].astype(F32))
            o_ref[pl.ds(r0, pair), vc] = (y * gate).astype(o_ref.dtype)
        return carry

    lax.fori_loop(0, rows // pair, pair_body, 0)


def _gla(proj, z, gate_w, gate_b, norm_g, l, B, S, rows):
    T = B * S
    nsteps = S // rows
    cq = OFF_GQ // GLA_QK_WIDTH
    ck = OFF_GK // GLA_QK_WIDTH
    assert OFF_GV % 512 == 0 and OFF_GR % 512 == 0
    row = lambda b, c: b * nsteps + c
    half = GLA_WIDTH // 2
    return pl.pallas_call(
        functools.partial(_gla_kernel_split, rows=rows),
        grid=(B, nsteps),
        in_specs=[
            pl.BlockSpec((rows, GLA_QK_WIDTH), lambda b, c: (row(b, c), cq)),
            pl.BlockSpec((rows, GLA_QK_WIDTH), lambda b, c: (row(b, c), ck)),
            pl.BlockSpec((rows, half), lambda b, c: (row(b, c), OFF_GV // half)),
            pl.BlockSpec((rows, half), lambda b, c: (row(b, c), OFF_GV // half + 1)),
            pl.BlockSpec((rows, half), lambda b, c: (row(b, c), OFF_GR // half)),
            pl.BlockSpec((rows, half), lambda b, c: (row(b, c), OFF_GR // half + 1)),
            pl.BlockSpec((rows, LANES), lambda b, c: (row(b, c), 0)),
            pl.BlockSpec((None, LANES, GLA_QK_WIDTH), lambda b, c: (l, 0, 0)),
            pl.BlockSpec((None, 1, GLA_QK_WIDTH), lambda b, c: (l, 0, 0)),
            pl.BlockSpec((None, 1, GLA_DV), lambda b, c: (l, 0, 0)),
        ],
        out_specs=pl.BlockSpec((rows, GLA_WIDTH), lambda b, c: (row(b, c), 0)),
        out_shape=jax.ShapeDtypeStruct((T, GLA_WIDTH), BF16),
        scratch_shapes=[
            pltpu.VMEM((GLA_HEADS, GLA_DK, GLA_DV), F32),
            pltpu.VMEM((rows, GLA_QK_WIDTH), F32),
        ],
        compiler_params=_params(2),
        name="gla",
    )(proj, proj, proj, proj, proj, proj, z, gate_w, gate_b, norm_g)


class _TwoHalves:
    def __init__(self, lo, hi, half):
        self.lo, self.hi, self.half = lo, hi, half

    def __getitem__(self, idx):
        rows, cols = idx
        if cols.start >= self.half:
            return self.hi[rows, cols.start - self.half:cols.stop - self.half]
        assert cols.stop <= self.half
        return self.lo[rows, cols]


def _gla_kernel_split(q_ref, k_ref, v0_ref, v1_ref, r0_ref, r1_ref, z_ref, gw_ref, gb_ref, ng_ref,
                      o_ref, s_ref, la_ref, *, rows):
    half = GLA_WIDTH // 2
    _gla_kernel(q_ref, k_ref, _TwoHalves(v0_ref, v1_ref, half), _TwoHalves(r0_ref, r1_ref, half),
                z_ref, gw_ref, gb_ref, ng_ref, o_ref, s_ref, la_ref, rows=rows)


def _out_proj_kernel(x_ref, a_ref, g_ref, w_ref, o_ref):
    acc = jnp.dot(a_ref[...], w_ref[0:ATTN_WIDTH, :], preferred_element_type=F32)
    acc = acc + jnp.dot(g_ref[...], w_ref[ATTN_WIDTH:, :], preferred_element_type=F32)
    o_ref[...] = x_ref[...] + acc


def _out_proj(x2, o_attn, o_gla, w_out_b, l, tm):
    T = x2.shape[0]
    return pl.pallas_call(
        _out_proj_kernel,
        grid=(T // tm,),
        in_specs=[
            pl.BlockSpec((tm, D_MODEL), lambda i: (i, 0)),
            pl.BlockSpec((tm, ATTN_WIDTH), lambda i: (i, 0)),
            pl.BlockSpec((tm, GLA_WIDTH), lambda i: (i, 0)),
            pl.BlockSpec((None, ATTN_WIDTH + GLA_WIDTH, D_MODEL), lambda i: (l, 0, 0)),
        ],
        out_specs=pl.BlockSpec((tm, D_MODEL), lambda i: (i, 0)),
        out_shape=jax.ShapeDtypeStruct((T, D_MODEL), F32),
        compiler_params=_params(1),
        name="out_proj",
    )(x2, o_attn, o_gla, w_out_b)


def _ffn_kernel(x_ref, g_ref, wu_ref, wd_ref, o_ref, hn_ref, *, row_chunk):
    f = pl.program_id(1)

    @pl.when(f == 0)
    def _():
        def body(r, carry):
            r0 = pl.multiple_of(r * row_chunk, row_chunk)
            xr = x_ref[pl.ds(r0, row_chunk), :]
            hn_ref[pl.ds(r0, row_chunk), :] = _rmsnorm_rows(xr, g_ref[...]).astype(BF16)
            o_ref[pl.ds(r0, row_chunk), :] = xr
            return carry
        lax.fori_loop(0, x_ref.shape[0] // row_chunk, body, 0)

    u = jnp.dot(hn_ref[...], wu_ref[...], preferred_element_type=F32)
    u = jnp.maximum(u, 0.0)
    u = (u * u).astype(BF16)
    o_ref[...] += jnp.dot(u, wd_ref[...], preferred_element_type=F32)


def _ffn(x2, g, w_up_b, w_down_b, l, tm, tf):
    T = x2.shape[0]
    return pl.pallas_call(
        functools.partial(_ffn_kernel, row_chunk=min(128, tm)),
        grid=(T // tm, D_FF // tf),
        in_specs=[
            pl.BlockSpec((tm, D_MODEL), lambda i, f: (i, 0)),
            pl.BlockSpec((None, 1, D_MODEL), lambda i, f: (l, 0, 0)),
            pl.BlockSpec((None, D_MODEL, tf), lambda i, f: (l, 0, f)),
            pl.BlockSpec((None, tf, D_MODEL), lambda i, f: (l, f, 0)),
        ],
        out_specs=pl.BlockSpec((tm, D_MODEL), lambda i, f: (i, 0)),
        out_shape=jax.ShapeDtypeStruct((T, D_MODEL), F32),
        scratch_shapes=[pltpu.VMEM((tm, D_MODEL), BF16)],
        compiler_params=_params(2),
        name="ffn",
    )(x2, g, w_up_b, w_down_b)


def _tile_plan(T, S):
    tm = min(1024, T)
    tm_out = min(512, T)
    mix_rows = min(512, S)
    assert T % tm == 0 and S % mix_rows == 0 and mix_rows % (2 * GLA_CHUNK) == 0
    return dict(tm=tm, tm_out=tm_out, tn_in=1536, tf=1024, mix_rows=mix_rows)


def kernel(x, norm1_g, w_in, q_norm_g, k_norm_g, attn_sinks, gla_gate_w, gla_gate_b, gla_norm_g,
           w_out, norm2_g, w_up, w_down):
    B, S, D = x.shape
    assert D == D_MODEL
    T = B * S
    plan = _tile_plan(T, S)
    depth = w_in.shape[0]
    x2 = x.reshape(T, D)

    w_in_b = w_in.astype(BF16)
    w_z = jnp.pad(w_in[:, :, MAIN_WIDTH:], ((0, 0), (0, 0), (0, LANES - GLA_RANK))).astype(BF16)
    gate_w = jnp.pad(gla_gate_w, ((0, 0), (0, LANES - GLA_RANK), (0, 0))).astype(BF16)
    w_out_b = w_out.astype(BF16)
    w_up_b = w_up.astype(BF16)
    w_down_b = w_down.astype(BF16)
    norm1 = norm1_g.reshape(depth, 1, D)
    norm2 = norm2_g.reshape(depth, 1, D)
    gate_b = gla_gate_b.reshape(depth, 1, GLA_QK_WIDTH)
    gla_g = gla_norm_g.reshape(depth, 1, GLA_DV)

    for l in range(depth):
        gq = jnp.tile(q_norm_g[l], 2).reshape(1, LANES)
        gk = jnp.tile(k_norm_g[l], 2).reshape(1, LANES)

        proj, z = _in_proj(x2, norm1, w_in_b, w_z, l, plan["tm"], plan["tn_in"])
        o_attn = _swa(proj, attn_sinks[l], gq, gk, B, S, plan["mix_rows"])
        o_gla = _gla(proj, z, gate_w, gate_b, gla_g, l, B, S, plan["mix_rows"])
        x2 = _out_proj(x2, o_attn, o_gla, w_out_b, l, plan["tm_out"])
        x2 = _ffn(x2, norm2, w_up_b, w_down_b, l, plan["tm"], plan["tf"])
    return x2.reshape(B, S, D)
```

```python
import functools

import jax
import jax.numpy as jnp
from jax import lax
from jax.experimental import pallas as pl
from jax.experimental.pallas import tpu as pltpu

F32 = jnp.float32
BF16 = jnp.bfloat16

D_MODEL = 2048
HEAD_DIM = 64
N_Q_HEADS = 16
N_KV_HEADS = 4
ATTN_BLOCK = 128
ATTN_WIDTH = N_Q_HEADS * HEAD_DIM
KV_WIDTH = N_KV_HEADS * HEAD_DIM
GLA_HEADS = 4
GLA_DK = 128
GLA_DV = 256
GLA_QK_WIDTH = GLA_HEADS * GLA_DK
GLA_WIDTH = GLA_HEADS * GLA_DV
GLA_RANK = 16
GLA_TAU = 16.0
GLA_CHUNK = 64
D_FF = 4 * D_MODEL
EPS = 1e-6
NEG = -1e30
LOG2E = 1.4426950408889634

LANES = 128
MAIN_WIDTH = 4608
OFF_AQ, OFF_AK, OFF_AV = 0, 1024, 1280
OFF_GQ, OFF_GK, OFF_GV, OFF_GR = 1536, 2048, 2560, 3584

VMEM_LIMIT = 60 * 1024 * 1024

ALIBI_SLOPES =tuple(2.0 ** (-8.0 * (i + 1) / N_Q_HEADS) for i in range(N_Q_HEADS))


def _params(n_axes, flags=None):
    return pltpu.CompilerParams(dimension_semantics=("arbitrary",) * n_axes,
                                vmem_limit_bytes=VMEM_LIMIT, flags=flags)


def _rmsnorm_rows(x, g):
    ms = jnp.mean(x * x, axis=-1, keepdims=True)
    return (x * lax.rsqrt(ms + EPS)) * g


def _in_proj_kernel(x_ref, g_ref, w_ref, wz_ref, o_ref, z_ref, hn_ref, *, row_chunk):
    j = pl.program_id(1)

    @pl.when(j == 0)
    def _():
        def body(r, carry):
            r0 = pl.multiple_of(r * row_chunk, row_chunk)
            hn = _rmsnorm_rows(x_ref[pl.ds(r0, row_chunk), :], g_ref[...])
            hn_ref[pl.ds(r0, row_chunk), :] = hn.astype(BF16)
            return carry
        lax.fori_loop(0, x_ref.shape[0] // row_chunk, body, 0)
        z_ref[...] = jnp.dot(hn_ref[...], wz_ref[...], preferred_element_type=F32)

    o_ref[...] = jnp.dot(hn_ref[...], w_ref[...], preferred_element_type=F32).astype(o_ref.dtype)


def _in_proj(x2, g, w_in_b, w_z, l, tm, tn):
    T = x2.shape[0]
    grid = (T // tm, MAIN_WIDTH // tn)
    return pl.pallas_call(
        functools.partial(_in_proj_kernel, row_chunk=min(128, tm)),
        grid=grid,
        in_specs=[
            pl.BlockSpec((tm, D_MODEL), lambda i, j: (i, 0)),
            pl.BlockSpec((None, 1, D_MODEL), lambda i, j: (l, 0, 0)),
            pl.BlockSpec((D_MODEL, tn), lambda i, j: (0, j)),
            pl.BlockSpec((None, D_MODEL, LANES), lambda i, j: (l, 0, 0)),
        ],
        out_specs=[
            pl.BlockSpec((tm, tn), lambda i, j: (i, j)),
            pl.BlockSpec((tm, LANES), lambda i, j: (i, 0)),
        ],
        out_shape=[
            jax.ShapeDtypeStruct((T, MAIN_WIDTH), BF16),
            jax.ShapeDtypeStruct((T, LANES), F32),
        ],
        scratch_shapes=[pltpu.VMEM((tm, D_MODEL), BF16)],
        compiler_params=_params(2),
        name="in_proj",
    )(x2, g, w_in_b, w_z)


def _head_sumsq(xt):
    r = lax.broadcasted_iota(jnp.int32, (LANES, LANES), 0) // HEAD_DIM
    c = lax.broadcasted_iota(jnp.int32, (LANES, LANES), 1) // HEAD_DIM
    ones_blockdiag = (r == c).astype(BF16)
    return jnp.dot((xt * xt).astype(BF16), ones_blockdiag, preferred_element_type=F32)


def _head_rmsnorm(xt, g):
    return xt * lax.rsqrt(_head_sumsq(xt) * (1.0 / HEAD_DIM) + EPS) * g


def _swa_kernel(sink_ref, q_ref, k_ref, v_ref, gq_ref, gk_ref, wu_ref, wd_ref,
                o_ref, wub_ref, wdb_ref,
                qn_ref, kk_ref, vv_ref, tbl_ref, s0_ref, s1_ref, *, rows):
    b = pl.program_id(0)
    n = pl.program_id(1)
    blk = ATTN_BLOCK
    low = lax.broadcasted_iota(jnp.int32, (1, LANES), 1) < HEAD_DIM

    @pl.when(jnp.logical_and(b == 0, n == 0))
    def _():
        qi = lax.broadcasted_iota(jnp.int32, (blk, 2 * blk), 0)
        kj = lax.broadcasted_iota(jnp.int32, (blk, 2 * blk), 1)
        dist = qi + blk - kj
        valid = jnp.logical_and(dist >= 0, dist < blk)
        distf = dist.astype(F32)
        for qh in range(N_Q_HEADS):
            tbl_ref[qh] = jnp.where(valid, (-ALIBI_SLOPES[qh] * LOG2E) * distf, NEG)

    @pl.when(n == 0)
    def _():
        kk_ref[:, 0:blk, :] = jnp.zeros((2 * N_KV_HEADS, blk, LANES), BF16)
        vv_ref[:, 0:blk, :] = jnp.zeros((2 * N_KV_HEADS, blk, LANES), BF16)

    for t in range(ATTN_WIDTH // LANES):
        qt = q_ref[:, t * LANES:(t + 1) * LANES].astype(F32)
        qn = _head_rmsnorm(qt, gq_ref[...]) * (HEAD_DIM ** -0.5 * LOG2E)
        qn_ref[:, t * LANES:(t + 1) * LANES] = qn.astype(BF16)

    for p in range(KV_WIDTH // LANES):
        kt = _head_rmsnorm(k_ref[:, p * LANES:(p + 1) * LANES].astype(F32), gk_ref[...])
        vt = v_ref[:, p * LANES:(p + 1) * LANES].astype(F32)
        k_even = jnp.where(low, kt, 0.0)
        k_odd = jnp.where(low, 0.0, kt)
        v_even = jnp.where(low, vt, 1.0)
        v_odd = jnp.where(low, 1.0, vt)
        h0, h1 = 2 * p, 2 * p + 1
        kk_ref[2 * h0 + 0, blk:, :] = k_even.astype(BF16)
        kk_ref[2 * h0 + 1, blk:, :] = pltpu.roll(k_even, HEAD_DIM, 1).astype(BF16)
        kk_ref[2 * h1 + 1, blk:, :] = k_odd.astype(BF16)
        kk_ref[2 * h1 + 0, blk:, :] = pltpu.roll(k_odd, HEAD_DIM, 1).astype(BF16)
        vv_ref[2 * h0 + 0, blk:, :] = v_even.astype(BF16)
        vv_ref[2 * h0 + 1, blk:, :] = pltpu.roll(v_even, HEAD_DIM, 1).astype(BF16)
        vv_ref[2 * h1 + 1, blk:, :] = v_odd.astype(BF16)
        vv_ref[2 * h1 + 0, blk:, :] = pltpu.roll(v_odd, HEAD_DIM, 1).astype(BF16)

    prev_cols = lax.broadcasted_iota(jnp.int32, (1, 2 * blk), 1) < blk

    def stage(nb_a, nb_b):
        for h in range(N_KV_HEADS):
            for tt in range(2):
                t = 2 * h + tt
                if nb_a is not None:
                    r0 = nb_a * blk
                    s_out = s_refs[nb_a % 2]
                    qs = qn_ref[r0:r0 + blk, t * LANES:(t + 1) * LANES]
                    for a in range(2):
                        qh = 4 * h + 2 * tt + a
                        keys = kk_ref[2 * h + a, r0:r0 + 2 * blk, :]
                        s = lax.dot_general(qs, keys, (((1,), (1,)), ((), ())),
                                            preferred_element_type=F32) + tbl_ref[qh]
                        if nb_a == 0:
                            s = s + negrow
                        s_out[qh] = s
                if nb_b is not None:
                    r0 = nb_b * blk
                    s_in = s_refs[nb_b % 2]
                    res, sink_terms = [], []
                    for a in range(2):
                        qh = 4 * h + 2 * tt + a
                        s = s_in[qh]
                        sink = sink_ref[qh] * LOG2E
                        m = jnp.maximum(jnp.max(s, axis=-1, keepdims=True), sink)
                        p = jnp.exp2(s - m).astype(BF16)
                        vals = vv_ref[2 * h + a, r0:r0 + 2 * blk, :]
                        res.append(jnp.dot(p, vals, preferred_element_type=F32))
                        sink_terms.append(jnp.exp2(sink - m))
                    num = jnp.where(low, res[0], res[1])
                    den = pltpu.roll(jnp.where(low, res[1], res[0]), HEAD_DIM, 1)
                    den = den + jnp.where(low, sink_terms[0], sink_terms[1])
                    o_ref[r0:r0 + blk, t * LANES:(t + 1) * LANES] = (num / den).astype(o_ref.dtype)

    s_refs = (s0_ref, s1_ref)
    negrow = jnp.where(jnp.logical_and(n == 0, prev_cols), NEG, 0.0).astype(F32)
    nblk = rows // blk
    stage(0, None)
    for nb in range(nblk - 1):
        stage(nb + 1, nb)
    stage(None, nblk - 1)

    kk_ref[:, 0:blk, :] = kk_ref[:, rows:rows + blk, :]
    vv_ref[:, 0:blk, :] = vv_ref[:, rows:rows + blk, :]

    wub_ref[...] = wu_ref[...].astype(BF16)
    wdb_ref[...] = wd_ref[...].astype(BF16)


def _swa(proj, sinks, gq, gk, w_up, w_down, l, B, S, rows):
    T = B * S
    nsteps = S // rows
    total_steps = B * nsteps
    up_rows = D_MODEL // total_steps
    down_rows = D_FF // total_steps
    assert up_rows % 16 == 0 and up_rows * total_steps == D_MODEL and down_rows * total_steps == D_FF
    qcol = OFF_AQ // ATTN_WIDTH
    kcol = OFF_AK // KV_WIDTH
    vcol = OFF_AV // KV_WIDTH
    return pl.pallas_call(
        functools.partial(_swa_kernel, rows=rows),
        grid=(B, nsteps),
        in_specs=[
            pl.BlockSpec(memory_space=pltpu.SMEM),
            pl.BlockSpec((rows, ATTN_WIDTH), lambda b, n: (b * nsteps + n, qcol)),
            pl.BlockSpec((rows, KV_WIDTH), lambda b, n: (b * nsteps + n, kcol)),
            pl.BlockSpec((rows, KV_WIDTH), lambda b, n: (b * nsteps + n, vcol)),
            pl.BlockSpec((1, LANES), lambda b, n: (0, 0)),
            pl.BlockSpec((1, LANES), lambda b, n: (0, 0)),
            pl.BlockSpec((None, up_rows, D_FF), lambda b, n: (l, b * nsteps + n, 0)),
            pl.BlockSpec((None, down_rows, D_MODEL), lambda b, n: (l, b * nsteps + n, 0)),
        ],
        out_specs=[
            pl.BlockSpec((rows, ATTN_WIDTH), lambda b, n: (b * nsteps + n, 0)),
            pl.BlockSpec((up_rows, D_FF), lambda b, n: (b * nsteps + n, 0)),
            pl.BlockSpec((down_rows, D_MODEL), lambda b, n: (b * nsteps + n, 0)),
        ],
        out_shape=[
            jax.ShapeDtypeStruct((T, ATTN_WIDTH), BF16),
            jax.ShapeDtypeStruct((D_MODEL, D_FF), BF16),
            jax.ShapeDtypeStruct((D_FF, D_MODEL), BF16),
        ],
        scratch_shapes=[
            pltpu.VMEM((rows, ATTN_WIDTH), BF16),
            pltpu.VMEM((2 * N_KV_HEADS, rows + ATTN_BLOCK, LANES), BF16),
            pltpu.VMEM((2 * N_KV_HEADS, rows + ATTN_BLOCK, LANES), BF16),
            pltpu.VMEM((N_Q_HEADS, ATTN_BLOCK, 2 * ATTN_BLOCK), F32),
            pltpu.VMEM((N_Q_HEADS, ATTN_BLOCK, 2 * ATTN_BLOCK), F32),
            pltpu.VMEM((N_Q_HEADS, ATTN_BLOCK, 2 * ATTN_BLOCK), F32),
        ],
        compiler_params=_params(2),
        name="swa",
    )(sinks, proj, proj, proj, gq, gk, w_up, w_down)


def _log_sigmoid(x):
    return jnp.minimum(x, 0.0) - jnp.log(1.0 + jnp.exp(jnp.minimum(x, -x)))


def _silu(x):
    h = 0.5 * x
    return h * (1.0 + jnp.tanh(h))


def _gla_kernel(q_ref, k_ref, v_ref, r_ref, z_ref, gw_ref, gb_ref, ng_ref, o_ref,
                s_ref, la_ref, *, rows):
    c = pl.program_id(1)
    pair = 2 * GLA_CHUNK

    @pl.when(c == 0)
    def _():
        s_ref[...] = jnp.zeros_like(s_ref)

    ri = lax.broadcasted_iota(jnp.int32, (pair, pair), 0)
    ci = lax.broadcasted_iota(jnp.int32, (pair, pair), 1)
    causal = jnp.logical_and(ri // GLA_CHUNK == ci // GLA_CHUNK, ci <= ri)
    tril = causal.astype(BF16)
    low = lax.broadcasted_iota(jnp.int32, (1, pair), 1) < GLA_CHUNK

    logit = jnp.dot(z_ref[...].astype(BF16), gw_ref[...], preferred_element_type=F32) + gb_ref[...]
    la_ref[...] = _log_sigmoid(logit) * (1.0 / GLA_TAU)

    for pr in range(rows // pair):
        rs = slice(pr * pair, (pr + 1) * pair)
        la = la_ref[rs, :]
        la_hi = la.astype(BF16)
        la_lo = (la - la_hi.astype(F32)).astype(BF16)
        bcum = (jnp.dot(tril, la_hi, preferred_element_type=F32)
                + jnp.dot(tril, la_lo, preferred_element_type=F32))
        q = q_ref[rs, :].astype(F32) * (GLA_DK ** -0.5)
        k = k_ref[rs, :].astype(F32)
        q_in = (q * jnp.exp(bcum)).astype(BF16)
        k_in = (k * jnp.exp(-bcum)).astype(BF16)

        for h in range(GLA_HEADS):
            kc = slice(h * GLA_DK, (h + 1) * GLA_DK)
            vc = slice(h * GLA_DV, (h + 1) * GLA_DV)
            v = v_ref[rs, vc]
            qh = q_in[:, kc]
            att = lax.dot_general(qh, k_in[:, kc], (((1,), (1,)), ((), ())),
                                  preferred_element_type=F32)
            att = jnp.where(causal, att, 0.0).astype(BF16)
            o_intra = jnp.dot(att, v, preferred_element_type=F32)

            bt = bcum[:, kc].T
            kt = k[:, kc].T
            last_a = bt[:, GLA_CHUNK - 1:GLA_CHUNK]
            last_b = bt[:, pair - 1:pair]
            kst = kt * jnp.exp(jnp.where(low, last_a, last_b) - bt)
            kst_a = jnp.where(low, kst, 0.0).astype(BF16)
            kst_b = jnp.where(low, 0.0, kst).astype(BF16)

            state = s_ref[h]
            o_a = jnp.dot(qh[0:GLA_CHUNK], state.astype(BF16), preferred_element_type=F32)
            state = state * jnp.exp(last_a) + jnp.dot(kst_a, v, preferred_element_type=F32)
            o_b = jnp.dot(qh[GLA_CHUNK:pair], state.astype(BF16), preferred_element_type=F32)
            state = state * jnp.exp(last_b) + jnp.dot(kst_b, v, preferred_element_type=F32)
            s_ref[h] = state

            o = o_intra + jnp.concatenate([o_a, o_b], axis=0)
            y = _rmsnorm_rows(o, ng_ref[...])
            gate = _silu(r_ref[rs, vc].astype(F32))
            o_ref[rs, vc] = (y * gate).astype(o_ref.dtype)


def _gla(proj, z, gate_w, gate_b, norm_g, w_in, l, B, S, rows):
    T = B * S
    nsteps = S // rows
    cast_next = l + 1 < w_in.shape[0]
    win_rows = D_MODEL // (B * nsteps)
    assert win_rows % 16 == 0 and win_rows * B * nsteps == D_MODEL
    in_width = w_in.shape[2]
    cq = OFF_GQ // GLA_QK_WIDTH
    ck = OFF_GK // GLA_QK_WIDTH
    assert OFF_GV % 512 == 0 and OFF_GR % 512 == 0
    row = lambda b, c: b * nsteps + c
    half = GLA_WIDTH // 2
    return pl.pallas_call(
        functools.partial(_gla_kernel_split, rows=rows, cast_next=cast_next),
        grid=(B, nsteps),
        in_specs=[
            pl.BlockSpec((rows, GLA_QK_WIDTH), lambda b, c: (row(b, c), cq)),
            pl.BlockSpec((rows, GLA_QK_WIDTH), lambda b, c: (row(b, c), ck)),
            pl.BlockSpec((rows, half), lambda b, c: (row(b, c), OFF_GV // half)),
            pl.BlockSpec((rows, half), lambda b, c: (row(b, c), OFF_GV // half + 1)),
            pl.BlockSpec((rows, half), lambda b, c: (row(b, c), OFF_GR // half)),
            pl.BlockSpec((rows, half), lambda b, c: (row(b, c), OFF_GR // half + 1)),
            pl.BlockSpec((rows, LANES), lambda b, c: (row(b, c), 0)),
            pl.BlockSpec((None, LANES, GLA_QK_WIDTH), lambda b, c: (l, 0, 0)),
            pl.BlockSpec((None, 1, GLA_QK_WIDTH), lambda b, c: (l, 0, 0)),
            pl.BlockSpec((None, 1, GLA_DV), lambda b, c: (l, 0, 0)),
        ] + ([pl.BlockSpec((None, win_rows, in_width), lambda b, c: (l + 1, row(b, c), 0))] if cast_next else []),
        out_specs=[pl.BlockSpec((rows, GLA_WIDTH), lambda b, c: (row(b, c), 0))]
        + ([pl.BlockSpec((win_rows, in_width), lambda b, c: (row(b, c), 0))] if cast_next else []),
        out_shape=[jax.ShapeDtypeStruct((T, GLA_WIDTH), BF16)]
        + ([jax.ShapeDtypeStruct((D_MODEL, in_width), BF16)] if cast_next else []),
        scratch_shapes=[
            pltpu.VMEM((GLA_HEADS, GLA_DK, GLA_DV), F32),
            pltpu.VMEM((rows, GLA_QK_WIDTH), F32),
        ],
        compiler_params=_params(2),
        name="gla",
    )(proj, proj, proj, proj, proj, proj, z, gate_w, gate_b, norm_g, *([w_in] if cast_next else []))


class _TwoHalves:
    def __init__(self, lo, hi, half):
        self.lo, self.hi, self.half = lo, hi, half

    def __getitem__(self, idx):
        rows, cols = idx
        if cols.start >= self.half:
            return self.hi[rows, cols.start - self.half:cols.stop - self.half]
        assert cols.stop <= self.half
        return self.lo[rows, cols]


def _gla_kernel_split(q_ref, k_ref, v0_ref, v1_ref, r0_ref, r1_ref, z_ref, gw_ref, gb_ref, ng_ref,
                      *rest, rows, cast_next):
    if cast_next:
        win_ref, o_ref, winb_ref, *scratch = rest
        winb_ref[...] = win_ref[...].astype(BF16)
    else:
        o_ref, *scratch = rest
    half = GLA_WIDTH // 2
    _gla_kernel(q_ref, k_ref, _TwoHalves(v0_ref, v1_ref, half), _TwoHalves(r0_ref, r1_ref, half),
                z_ref, gw_ref, gb_ref, ng_ref, o_ref, *scratch, rows=rows)


def _out_proj_kernel(x_ref, a_ref, g_ref, w_ref, o_ref):
    acc = jnp.dot(a_ref[...], w_ref[0:ATTN_WIDTH, :], preferred_element_type=F32)
    acc = acc + jnp.dot(g_ref[...], w_ref[ATTN_WIDTH:, :], preferred_element_type=F32)
    o_ref[...] = x_ref[...] + acc


def _out_proj(x2, o_attn, o_gla, w_out, l, tm):
    T = x2.shape[0]
    return pl.pallas_call(
        _out_proj_kernel,
        grid=(T // tm,),
        in_specs=[
            pl.BlockSpec((tm, D_MODEL), lambda i: (i, 0)),
            pl.BlockSpec((tm, ATTN_WIDTH), lambda i: (i, 0)),
            pl.BlockSpec((tm, GLA_WIDTH), lambda i: (i, 0)),
            pl.BlockSpec((None, ATTN_WIDTH + GLA_WIDTH, D_MODEL), lambda i: (l, 0, 0)),
        ],
        out_specs=pl.BlockSpec((tm, D_MODEL), lambda i: (i, 0)),
        out_shape=jax.ShapeDtypeStruct((T, D_MODEL), F32),
        compiler_params=_params(1),
        name="out_proj",
    )(x2, o_attn, o_gla, w_out)


def _ffn_kernel(x_ref, g_ref, wu_ref, wd_ref, o_ref, hn_ref, *, row_chunk):
    f = pl.program_id(1)

    @pl.when(f == 0)
    def _():
        def body(r, carry):
            r0 = pl.multiple_of(r * row_chunk, row_chunk)
            xr = x_ref[pl.ds(r0, row_chunk), :]
            hn_ref[pl.ds(r0, row_chunk), :] = _rmsnorm_rows(xr, g_ref[...]).astype(BF16)
            o_ref[pl.ds(r0, row_chunk), :] = xr
            return carry
        lax.fori_loop(0, x_ref.shape[0] // row_chunk, body, 0)

    u = jnp.dot(hn_ref[...], wu_ref[...], preferred_element_type=F32)
    u = jnp.maximum(u, 0.0)
    u = (u * u).astype(BF16)
    o_ref[...] += jnp.dot(u, wd_ref[...], preferred_element_type=F32)


def _ffn(x2, g, w_up_b, w_down_b, l, tm, tf):
    T = x2.shape[0]
    return pl.pallas_call(
        functools.partial(_ffn_kernel, row_chunk=min(128, tm)),
        grid=(T // tm, D_FF // tf),
        in_specs=[
            pl.BlockSpec((tm, D_MODEL), lambda i, f: (i, 0)),
            pl.BlockSpec((None, 1, D_MODEL), lambda i, f: (l, 0, 0)),
            pl.BlockSpec((D_MODEL, tf), lambda i, f: (0, f)),
            pl.BlockSpec((tf, D_MODEL), lambda i, f: (f, 0)),
        ],
        out_specs=pl.BlockSpec((tm, D_MODEL), lambda i, f: (i, 0)),
        out_shape=jax.ShapeDtypeStruct((T, D_MODEL), F32),
        scratch_shapes=[pltpu.VMEM((tm, D_MODEL), BF16)],
        compiler_params=_params(2),
        name="ffn",
    )(x2, g, w_up_b, w_down_b)


def _tile_plan(T, S):
    tm = min(1024, T)
    tm_out = min(512, T)
    mix_rows = min(512, S)
    assert T % tm == 0 and S % mix_rows == 0 and mix_rows % (2 * GLA_CHUNK) == 0
    return dict(tm=tm, tm_out=tm_out, tn_in=1536, tf=1024, mix_rows=mix_rows)


def kernel(x, norm1_g, w_in, q_norm_g, k_norm_g, attn_sinks, gla_gate_w, gla_gate_b, gla_norm_g,
           w_out, norm2_g, w_up, w_down):
    B, S, D = x.shape
    assert D == D_MODEL
    T = B * S
    plan = _tile_plan(T, S)
    depth = w_in.shape[0]
    x2 = x.reshape(T, D)

    w_in_b = w_in[0].astype(BF16)
    w_z = jnp.pad(w_in[:, :, MAIN_WIDTH:], ((0, 0), (0, 0), (0, LANES - GLA_RANK))).astype(BF16)
    gate_w = jnp.pad(gla_gate_w, ((0, 0), (0, LANES - GLA_RANK), (0, 0))).astype(BF16)
    norm1 = norm1_g.reshape(depth, 1, D)
    norm2 = norm2_g.reshape(depth, 1, D)
    gate_b = gla_gate_b.reshape(depth, 1, GLA_QK_WIDTH)
    gla_g = gla_norm_g.reshape(depth, 1, GLA_DV)

    for l in range(depth):
        gq = jnp.tile(q_norm_g[l], 2).reshape(1, LANES)
        gk = jnp.tile(k_norm_g[l], 2).reshape(1, LANES)

        proj, z = _in_proj(x2, norm1, w_in_b, w_z, l, plan["tm"], plan["tn_in"])
        o_attn, w_up_b, w_down_b = _swa(proj, attn_sinks[l], gq, gk, w_up, w_down, l, B, S, plan["mix_rows"])
        o_gla, *w_in_next = _gla(proj, z, gate_w, gate_b, gla_g, w_in, l, B, S, plan["mix_rows"])
        x2 = _out_proj(x2, o_attn, o_gla, w_out, l, plan["tm_out"])
        x2 = _ffn(x2, norm2, w_up_b, w_down_b, l, plan["tm"], plan["tf"])
        if w_in_next:
            w_in_b = w_in_next[0]
    return x2.reshape(B, S, D)
```

```python
import functools

import jax
import jax.numpy as jnp
from jax import lax
from jax.experimental import pallas as pl
from jax.experimental.pallas import tpu as pltpu

F32 = jnp.float32
BF16 = jnp.bfloat16

D_MODEL = 2048
HEAD_DIM = 64
N_Q_HEADS = 16
N_KV_HEADS = 4
ATTN_BLOCK = 128
ATTN_WIDTH = N_Q_HEADS * HEAD_DIM
KV_WIDTH = N_KV_HEADS * HEAD_DIM
GLA_HEADS = 4
GLA_DK = 128
GLA_DV = 256
GLA_QK_WIDTH = GLA_HEADS * GLA_DK
GLA_WIDTH = GLA_HEADS * GLA_DV
GLA_RANK = 16
GLA_TAU = 16.0
GLA_CHUNK = 64
D_FF = 4 * D_MODEL
EPS = 1e-6
NEG = -1e30
LOG2E = 1.4426950408889634

LANES = 128
MAIN_WIDTH = 4608
OFF_AQ, OFF_AK, OFF_AV = 0, 1024, 1280
OFF_GQ, OFF_GK, OFF_GV, OFF_GR = 1536, 2048, 2560, 3584

VMEM_LIMIT = 60 * 1024 * 1024

ALIBI_SLOPES =tuple(2.0 ** (-8.0 * (i + 1) / N_Q_HEADS) for i in range(N_Q_HEADS))


def _params(n_axes, flags=None):
    return pltpu.CompilerParams(dimension_semantics=("arbitrary",) * n_axes,
                                vmem_limit_bytes=VMEM_LIMIT, flags=flags)


def _rmsnorm_rows(x, g):
    ms = jnp.mean(x * x, axis=-1, keepdims=True)
    return (x * lax.rsqrt(ms + EPS)) * g


def _in_proj_kernel(x_ref, g_ref, w_ref, wz_ref, o_ref, z_ref):
    hn = _rmsnorm_rows(x_ref[...], g_ref[...]).astype(BF16)
    o_ref[...] = jnp.dot(hn, w_ref[...], preferred_element_type=F32).astype(o_ref.dtype)

    @pl.when(pl.program_id(0) == 0)
    def _():
        z_ref[...] = jnp.dot(hn, wz_ref[...], preferred_element_type=F32)


def _in_proj(x2, g, w_in_b, w_z, l, tm, tn):
    T = x2.shape[0]
    n_rows = T // tm
    z_index = lambda j, i: (jnp.where(j == 0, i, n_rows - 1), 0)
    return pl.pallas_call(
        _in_proj_kernel,
        grid=(MAIN_WIDTH // tn, n_rows),
        in_specs=[
            pl.BlockSpec((tm, D_MODEL), lambda j, i: (i, 0)),
            pl.BlockSpec((None, 1, D_MODEL), lambda j, i: (l, 0, 0)),
            pl.BlockSpec((None, D_MODEL, tn), lambda j, i: (l, 0, j)),
            pl.BlockSpec((None, D_MODEL, LANES), lambda j, i: (l, 0, 0)),
        ],
        out_specs=[
            pl.BlockSpec((tm, tn), lambda j, i: (i, j)),
            pl.BlockSpec((tm, LANES), z_index),
        ],
        out_shape=[
            jax.ShapeDtypeStruct((T, MAIN_WIDTH), BF16),
            jax.ShapeDtypeStruct((T, LANES), F32),
        ],
        compiler_params=_params(2),
        name="in_proj",
    )(x2, g, w_in_b, w_z)


def _head_meansq(xt):
    r = lax.broadcasted_iota(jnp.int32, (LANES, LANES), 0) // HEAD_DIM
    c = lax.broadcasted_iota(jnp.int32, (LANES, LANES), 1) // HEAD_DIM
    mean_blockdiag = jnp.where(r == c, 1.0 / HEAD_DIM, 0.0).astype(BF16)
    return jnp.dot((xt * xt).astype(BF16), mean_blockdiag, preferred_element_type=F32)


def _head_rmsnorm(xt, g):
    return xt * lax.rsqrt(_head_meansq(xt) + EPS) * g


def _swa_kernel(sink_ref, q_ref, k_ref, v_ref, gq_ref, gk_ref, wu_ref, wd_ref,
                o_ref, wub_ref, wdb_ref,
                qn_ref, kk_ref, vv_ref, tbl_ref, s0_ref, s1_ref, *, rows):
    b = pl.program_id(0)
    n = pl.program_id(1)
    blk = ATTN_BLOCK
    low = lax.broadcasted_iota(jnp.int32, (1, LANES), 1) < HEAD_DIM

    @pl.when(jnp.logical_and(b == 0, n == 0))
    def _():
        qi = lax.broadcasted_iota(jnp.int32, (blk, 2 * blk), 0)
        kj = lax.broadcasted_iota(jnp.int32, (blk, 2 * blk), 1)
        dist = qi + blk - kj
        valid = jnp.logical_and(dist >= 0, dist < blk)
        distf = dist.astype(F32)
        for qh in range(N_Q_HEADS):
            tbl_ref[qh] = jnp.where(valid, (-ALIBI_SLOPES[qh] * LOG2E) * distf, NEG)

    @pl.when(n == 0)
    def _():
        kk_ref[:, 0:blk, :] = jnp.zeros((2 * N_KV_HEADS, blk, LANES), BF16)
        vv_ref[:, 0:blk, :] = jnp.zeros((2 * N_KV_HEADS, blk, LANES), BF16)

    for t in range(ATTN_WIDTH // LANES):
        qt = q_ref[:, t * LANES:(t + 1) * LANES].astype(F32)
        qn = _head_rmsnorm(qt, gq_ref[...])
        qn_ref[:, t * LANES:(t + 1) * LANES] = qn.astype(BF16)

    for p in range(KV_WIDTH // LANES):
        kt = _head_rmsnorm(k_ref[:, p * LANES:(p + 1) * LANES].astype(F32), gk_ref[...])
        vt = v_ref[:, p * LANES:(p + 1) * LANES].astype(F32)
        k_even = jnp.where(low, kt, 0.0)
        k_odd = jnp.where(low, 0.0, kt)
        v_even = jnp.where(low, vt, 1.0)
        v_odd = jnp.where(low, 1.0, vt)
        h0, h1 = 2 * p, 2 * p + 1
        kk_ref[2 * h0 + 0, blk:, :] = k_even.astype(BF16)
        kk_ref[2 * h0 + 1, blk:, :] = pltpu.roll(k_even, HEAD_DIM, 1).astype(BF16)
        kk_ref[2 * h1 + 1, blk:, :] = k_odd.astype(BF16)
        kk_ref[2 * h1 + 0, blk:, :] = pltpu.roll(k_odd, HEAD_DIM, 1).astype(BF16)
        vv_ref[2 * h0 + 0, blk:, :] = v_even.astype(BF16)
        vv_ref[2 * h0 + 1, blk:, :] = pltpu.roll(v_even, HEAD_DIM, 1).astype(BF16)
        vv_ref[2 * h1 + 1, blk:, :] = v_odd.astype(BF16)
        vv_ref[2 * h1 + 0, blk:, :] = pltpu.roll(v_odd, HEAD_DIM, 1).astype(BF16)

    prev_cols = lax.broadcasted_iota(jnp.int32, (1, 2 * blk), 1) < blk

    def stage(nb_a, nb_b):
        for h in range(N_KV_HEADS):
            for tt in range(2):
                t = 2 * h + tt
                if nb_a is not None:
                    r0 = nb_a * blk
                    s_out = s_refs[nb_a % 2]
                    qs = qn_ref[r0:r0 + blk, t * LANES:(t + 1) * LANES]
                    for a in range(2):
                        qh = 4 * h + 2 * tt + a
                        keys = kk_ref[2 * h + a, r0:r0 + 2 * blk, :]
                        s_out[qh] = lax.dot_general(qs, keys, (((1,), (1,)), ((), ())),
                                                    preferred_element_type=F32)
                if nb_b is not None:
                    r0 = nb_b * blk
                    s_in = s_refs[nb_b % 2]
                    res, sink_terms = [], []
                    for a in range(2):
                        qh = 4 * h + 2 * tt + a
                        s = s_in[qh] + tbl_ref[qh]
                        if nb_b == 0:
                            s = s + negrow
                        sink = sink_ref[qh] * LOG2E
                        m = jnp.maximum(jnp.max(s, axis=-1, keepdims=True), sink)
                        p = jnp.exp2(s - m).astype(BF16)
                        vals = vv_ref[2 * h + a, r0:r0 + 2 * blk, :]
                        res.append(jnp.dot(p, vals, preferred_element_type=F32))
                        sink_terms.append(jnp.exp2(sink - m))
                    num = jnp.where(low, res[0], res[1])
                    den = pltpu.roll(jnp.where(low, res[1], res[0]), HEAD_DIM, 1)
                    den = den + jnp.where(low, sink_terms[0], sink_terms[1])
                    o_ref[r0:r0 + blk, t * LANES:(t + 1) * LANES] = (num / den).astype(o_ref.dtype)

    s_refs = (s0_ref, s1_ref)
    negrow = jnp.where(jnp.logical_and(n == 0, prev_cols), NEG, 0.0).astype(F32)
    nblk = rows // blk
    stage(0, None)
    for nb in range(nblk - 1):
        stage(nb + 1, nb)
    stage(None, nblk - 1)

    kk_ref[:, 0:blk, :] = kk_ref[:, rows:rows + blk, :]
    vv_ref[:, 0:blk, :] = vv_ref[:, rows:rows + blk, :]

    wub_ref[...] = wu_ref[...].astype(BF16)
    wdb_ref[...] = wd_ref[...].astype(BF16)


def _swa(proj, sinks, gq, gk, w_up, w_down, l, B, S, rows):
    T = B * S
    nsteps = S // rows
    total_steps = B * nsteps
    up_rows = D_MODEL // total_steps
    down_rows = D_FF // total_steps
    assert up_rows % 16 == 0 and up_rows * total_steps == D_MODEL and down_rows * total_steps == D_FF
    qcol = OFF_AQ // ATTN_WIDTH
    kcol = OFF_AK // KV_WIDTH
    vcol = OFF_AV // KV_WIDTH
    return pl.pallas_call(
        functools.partial(_swa_kernel, rows=rows),
        grid=(B, nsteps),
        in_specs=[
            pl.BlockSpec(memory_space=pltpu.SMEM),
            pl.BlockSpec((rows, ATTN_WIDTH), lambda b, n: (b * nsteps + n, qcol)),
            pl.BlockSpec((rows, KV_WIDTH), lambda b, n: (b * nsteps + n, kcol)),
            pl.BlockSpec((rows, KV_WIDTH), lambda b, n: (b * nsteps + n, vcol)),
            pl.BlockSpec((1, LANES), lambda b, n: (0, 0)),
            pl.BlockSpec((1, LANES), lambda b, n: (0, 0)),
            pl.BlockSpec((None, up_rows, D_FF), lambda b, n: (l, b * nsteps + n, 0)),
            pl.BlockSpec((None, down_rows, D_MODEL), lambda b, n: (l, b * nsteps + n, 0)),
        ],
        out_specs=[
            pl.BlockSpec((rows, ATTN_WIDTH), lambda b, n: (b * nsteps + n, 0)),
            pl.BlockSpec((up_rows, D_FF), lambda b, n: (b * nsteps + n, 0)),
            pl.BlockSpec((down_rows, D_MODEL), lambda b, n: (b * nsteps + n, 0)),
        ],
        out_shape=[
            jax.ShapeDtypeStruct((T, ATTN_WIDTH), BF16),
            jax.ShapeDtypeStruct((D_MODEL, D_FF), BF16),
            jax.ShapeDtypeStruct((D_FF, D_MODEL), BF16),
        ],
        scratch_shapes=[
            pltpu.VMEM((rows, ATTN_WIDTH), BF16),
            pltpu.VMEM((2 * N_KV_HEADS, rows + ATTN_BLOCK, LANES), BF16),
            pltpu.VMEM((2 * N_KV_HEADS, rows + ATTN_BLOCK, LANES), BF16),
            pltpu.VMEM((N_Q_HEADS, ATTN_BLOCK, 2 * ATTN_BLOCK), F32),
            pltpu.VMEM((N_Q_HEADS, ATTN_BLOCK, 2 * ATTN_BLOCK), F32),
            pltpu.VMEM((N_Q_HEADS, ATTN_BLOCK, 2 * ATTN_BLOCK), F32),
        ],
        compiler_params=_params(2),
        name="swa",
    )(sinks, proj, proj, proj, gq, gk, w_up, w_down)


def _log_sigmoid(x):
    return jnp.minimum(x, 0.0) - jnp.log(1.0 + jnp.exp(jnp.minimum(x, -x)))


def _silu(x):
    h = 0.5 * x
    return h * (1.0 + jnp.tanh(h))


def _gla_kernel(q_ref, k_ref, v_ref, r_ref, z_ref, gw_ref, gb_ref, ng_ref, o_ref,
                s_ref, la_ref, *, rows):
    c = pl.program_id(1)
    pair = 2 * GLA_CHUNK

    @pl.when(c == 0)
    def _():
        s_ref[...] = jnp.zeros_like(s_ref)

    ri = lax.broadcasted_iota(jnp.int32, (pair, pair), 0)
    ci = lax.broadcasted_iota(jnp.int32, (pair, pair), 1)
    causal = jnp.logical_and(ri // GLA_CHUNK == ci // GLA_CHUNK, ci <= ri)
    tril = causal.astype(BF16)
    low = lax.broadcasted_iota(jnp.int32, (1, pair), 1) < GLA_CHUNK

    logit = jnp.dot(z_ref[...].astype(BF16), gw_ref[...], preferred_element_type=F32) + gb_ref[...]
    la_ref[...] = _log_sigmoid(logit) * (1.0 / GLA_TAU)

    for pr in range(rows // pair):
        rs = slice(pr * pair, (pr + 1) * pair)
        la = la_ref[rs, :]
        la_hi = la.astype(BF16)
        la_lo = (la - la_hi.astype(F32)).astype(BF16)
        bcum = (jnp.dot(tril, la_hi, preferred_element_type=F32)
                + jnp.dot(tril, la_lo, preferred_element_type=F32))
        q = q_ref[rs, :].astype(F32) * (GLA_DK ** -0.5)
        k = k_ref[rs, :].astype(F32)
        q_in = (q * jnp.exp(bcum)).astype(BF16)
        k_in = (k * jnp.exp(-bcum)).astype(BF16)

        for h in range(GLA_HEADS):
            kc = slice(h * GLA_DK, (h + 1) * GLA_DK)
            vc = slice(h * GLA_DV, (h + 1) * GLA_DV)
            v = v_ref[rs, vc]
            qh = q_in[:, kc]
            att = lax.dot_general(qh, k_in[:, kc], (((1,), (1,)), ((), ())),
                                  preferred_element_type=F32)
            att = jnp.where(causal, att, 0.0).astype(BF16)
            o_intra = jnp.dot(att, v, preferred_element_type=F32)

            bt = bcum[:, kc].T
            kt = k[:, kc].T
            last_a = bt[:, GLA_CHUNK - 1:GLA_CHUNK]
            last_b = bt[:, pair - 1:pair]
            kst = kt * jnp.exp(jnp.where(low, last_a, last_b) - bt)
            kst_a = jnp.where(low, kst, 0.0).astype(BF16)
            kst_b = jnp.where(low, 0.0, kst).astype(BF16)

            state = s_ref[h]
            o_a = jnp.dot(qh[0:GLA_CHUNK], state.astype(BF16), preferred_element_type=F32)
            state = state * jnp.exp(last_a) + jnp.dot(kst_a, v, preferred_element_type=F32)
            o_b = jnp.dot(qh[GLA_CHUNK:pair], state.astype(BF16), preferred_element_type=F32)
            state = state * jnp.exp(last_b) + jnp.dot(kst_b, v, preferred_element_type=F32)
            s_ref[h] = state

            o = o_intra + jnp.concatenate([o_a, o_b], axis=0)
            y = _rmsnorm_rows(o, ng_ref[...])
            gate = _silu(r_ref[rs, vc].astype(F32))
            o_ref[rs, vc] = (y * gate).astype(o_ref.dtype)


def _gla(proj, z, gate_w, gate_b, norm_g, l, B, S, rows):
    T = B * S
    nsteps = S // rows
    cq = OFF_GQ // GLA_QK_WIDTH
    ck = OFF_GK // GLA_QK_WIDTH
    assert OFF_GV % 512 == 0 and OFF_GR % 512 == 0
    row = lambda b, c: b * nsteps + c
    half = GLA_WIDTH // 2
    return pl.pallas_call(
        functools.partial(_gla_kernel_split, rows=rows),
        grid=(B, nsteps),
        in_specs=[
            pl.BlockSpec((rows, GLA_QK_WIDTH), lambda b, c: (row(b, c), cq)),
            pl.BlockSpec((rows, GLA_QK_WIDTH), lambda b, c: (row(b, c), ck)),
            pl.BlockSpec((rows, half), lambda b, c: (row(b, c), OFF_GV // half)),
            pl.BlockSpec((rows, half), lambda b, c: (row(b, c), OFF_GV // half + 1)),
            pl.BlockSpec((rows, half), lambda b, c: (row(b, c), OFF_GR // half)),
            pl.BlockSpec((rows, half), lambda b, c: (row(b, c), OFF_GR // half + 1)),
            pl.BlockSpec((rows, LANES), lambda b, c: (row(b, c), 0)),
            pl.BlockSpec((None, LANES, GLA_QK_WIDTH), lambda b, c: (l, 0, 0)),
            pl.BlockSpec((None, 1, GLA_QK_WIDTH), lambda b, c: (l, 0, 0)),
            pl.BlockSpec((None, 1, GLA_DV), lambda b, c: (l, 0, 0)),
        ],
        out_specs=pl.BlockSpec((rows, GLA_WIDTH), lambda b, c: (row(b, c), 0)),
        out_shape=jax.ShapeDtypeStruct((T, GLA_WIDTH), BF16),
        scratch_shapes=[
            pltpu.VMEM((GLA_HEADS, GLA_DK, GLA_DV), F32),
            pltpu.VMEM((rows, GLA_QK_WIDTH), F32),
        ],
        compiler_params=_params(2),
        name="gla",
    )(proj, proj, proj, proj, proj, proj, z, gate_w, gate_b, norm_g)


class _TwoHalves:
    def __init__(self, lo, hi, half):
        self.lo, self.hi, self.half = lo, hi, half

    def __getitem__(self, idx):
        rows, cols = idx
        if cols.start >= self.half:
            return self.hi[rows, cols.start - self.half:cols.stop - self.half]
        assert cols.stop <= self.half
        return self.lo[rows, cols]


def _gla_kernel_split(q_ref, k_ref, v0_ref, v1_ref, r0_ref, r1_ref, z_ref, gw_ref, gb_ref, ng_ref,
                      o_ref, *scratch, rows):
    half = GLA_WIDTH // 2
    _gla_kernel(q_ref, k_ref, _TwoHalves(v0_ref, v1_ref, half), _TwoHalves(r0_ref, r1_ref, half),
                z_ref, gw_ref, gb_ref, ng_ref, o_ref, *scratch, rows=rows)


def _out_proj_kernel(x_ref, a_ref, g_ref, w_ref, o_ref):
    acc = jnp.dot(a_ref[...], w_ref[0:ATTN_WIDTH, :], preferred_element_type=F32)
    acc = acc + jnp.dot(g_ref[...], w_ref[ATTN_WIDTH:, :], preferred_element_type=F32)
    o_ref[...] = x_ref[...] + acc


def _out_proj(x2, o_attn, o_gla, w_out, l, tm):
    T = x2.shape[0]
    return pl.pallas_call(
        _out_proj_kernel,
        grid=(T // tm,),
        in_specs=[
            pl.BlockSpec((tm, D_MODEL), lambda i: (i, 0)),
            pl.BlockSpec((tm, ATTN_WIDTH), lambda i: (i, 0)),
            pl.BlockSpec((tm, GLA_WIDTH), lambda i: (i, 0)),
            pl.BlockSpec((None, ATTN_WIDTH + GLA_WIDTH, D_MODEL), lambda i: (l, 0, 0)),
        ],
        out_specs=pl.BlockSpec((tm, D_MODEL), lambda i: (i, 0)),
        out_shape=jax.ShapeDtypeStruct((T, D_MODEL), F32),
        compiler_params=_params(1),
        name="out_proj",
    )(x2, o_attn, o_gla, w_out)


def _ffn_kernel(x_ref, g_ref, wu_ref, wd_ref, o_ref, hn_ref, *, row_chunk):
    f = pl.program_id(1)

    @pl.when(f == 0)
    def _():
        def body(r, carry):
            r0 = pl.multiple_of(r * row_chunk, row_chunk)
            xr = x_ref[pl.ds(r0, row_chunk), :]
            hn_ref[pl.ds(r0, row_chunk), :] = _rmsnorm_rows(xr, g_ref[...]).astype(BF16)
            o_ref[pl.ds(r0, row_chunk), :] = xr
            return carry
        lax.fori_loop(0, x_ref.shape[0] // row_chunk, body, 0)

    u = jnp.dot(hn_ref[...], wu_ref[...], preferred_element_type=F32)
    u = jnp.maximum(u, 0.0)
    u = (u * u).astype(BF16)
    o_ref[...] += jnp.dot(u, wd_ref[...], preferred_element_type=F32)


def _ffn(x2, g, w_up_b, w_down_b, l, tm, tf):
    T = x2.shape[0]
    return pl.pallas_call(
        functools.partial(_ffn_kernel, row_chunk=min(128, tm)),
        grid=(T // tm, D_FF // tf),
        in_specs=[
            pl.BlockSpec((tm, D_MODEL), lambda i, f: (i, 0)),
            pl.BlockSpec((None, 1, D_MODEL), lambda i, f: (l, 0, 0)),
            pl.BlockSpec((D_MODEL, tf), lambda i, f: (0, f)),
            pl.BlockSpec((tf, D_MODEL), lambda i, f: (f, 0)),
        ],
        out_specs=pl.BlockSpec((tm, D_MODEL), lambda i, f: (i, 0)),
        out_shape=jax.ShapeDtypeStruct((T, D_MODEL), F32),
        scratch_shapes=[pltpu.VMEM((tm, D_MODEL), BF16)],
        compiler_params=_params(2),
        name="ffn",
    )(x2, g, w_up_b, w_down_b)


def _tile_plan(T, S):
    tm = min(1024, T)
    tm_out = min(512, T)
    mix_rows = min(512, S)
    assert T % tm == 0 and S % mix_rows == 0 and mix_rows % (2 * GLA_CHUNK) == 0
    return dict(tm=tm, tm_out=tm_out, tn_in=1536, tf=1024, mix_rows=mix_rows)


def kernel(x, norm1_g, w_in, q_norm_g, k_norm_g, attn_sinks, gla_gate_w, gla_gate_b, gla_norm_g,
           w_out, norm2_g, w_up, w_down):
    B, S, D = x.shape
    assert D == D_MODEL
    T = B * S
    plan = _tile_plan(T, S)
    depth = w_in.shape[0]
    x2 = x.reshape(T, D)

    w_in_b = w_in.astype(BF16)
    w_z = jnp.pad(w_in[:, :, MAIN_WIDTH:], ((0, 0), (0, 0), (0, LANES - GLA_RANK))).astype(BF16)
    gate_w = jnp.pad(gla_gate_w, ((0, 0), (0, LANES - GLA_RANK), (0, 0))).astype(BF16)
    norm1 = norm1_g.reshape(depth, 1, D)
    norm2 = norm2_g.reshape(depth, 1, D)
    gate_b = gla_gate_b.reshape(depth, 1, GLA_QK_WIDTH)
    gla_g = gla_norm_g.reshape(depth, 1, GLA_DV)

    for l in range(depth):
        gq = jnp.tile(q_norm_g[l] * (HEAD_DIM ** -0.5 * LOG2E), 2).reshape(1, LANES)
        gk = jnp.tile(k_norm_g[l], 2).reshape(1, LANES)

        proj, z = _in_proj(x2, norm1, w_in_b, w_z, l, plan["tm"], plan["tn_in"])
        o_attn, w_up_b, w_down_b = _swa(proj, attn_sinks[l], gq, gk, w_up, w_down, l, B, S, plan["mix_rows"])
        o_gla = _gla(proj, z, gate_w, gate_b, gla_g, l, B, S, plan["mix_rows"])
        x2 = _out_proj(x2, o_attn, o_gla, w_out, l, plan["tm_out"])
        x2 = _ffn(x2, norm2, w_up_b, w_down_b, l, plan["tm"], plan["tf"])
    return x2.reshape(B, S, D)
```

```python
import functools

import jax
import jax.numpy as jnp
from jax import lax
from jax.experimental import pallas as pl
from jax.experimental.pallas import tpu as pltpu

F32 = jnp.float32
BF16 = jnp.bfloat16

D_MODEL = 2048
HEAD_DIM = 64
N_Q_HEADS = 16
N_KV_HEADS = 4
ATTN_BLOCK = 128
ATTN_WIDTH = N_Q_HEADS * HEAD_DIM
KV_WIDTH = N_KV_HEADS * HEAD_DIM
GLA_HEADS = 4
GLA_DK = 128
GLA_DV = 256
GLA_QK_WIDTH = GLA_HEADS * GLA_DK
GLA_WIDTH = GLA_HEADS * GLA_DV
GLA_RANK = 16
GLA_TAU = 16.0
GLA_CHUNK = 64
D_FF = 4 * D_MODEL
EPS = 1e-6
NEG = -1e30
LOG2E = 1.4426950408889634

LANES = 128
OUT_SLICE = 256
MAIN_WIDTH = 4608
OFF_AQ, OFF_AK, OFF_AV = 0, 1024, 1280
OFF_GQ, OFF_GK, OFF_GV, OFF_GR = 1536, 2048, 2560, 3584

VMEM_LIMIT = 60 * 1024 * 1024

ALIBI_SLOPES =tuple(2.0 ** (-8.0 * (i + 1) / N_Q_HEADS) for i in range(N_Q_HEADS))


def _params(n_axes, flags=None):
    return pltpu.CompilerParams(dimension_semantics=("arbitrary",) * n_axes,
                                vmem_limit_bytes=VMEM_LIMIT, flags=flags)


def _rmsnorm_rows(x, g):
    ms = jnp.mean(x * x, axis=-1, keepdims=True)
    return (x * lax.rsqrt(ms + EPS)) * g


def _in_proj_kernel(x_ref, g_ref, w_ref, wz_ref, o_ref, z_ref):
    hn = _rmsnorm_rows(x_ref[...], g_ref[...]).astype(BF16)
    o_ref[...] = jnp.dot(hn, w_ref[...], preferred_element_type=F32).astype(o_ref.dtype)

    @pl.when(pl.program_id(0) == 0)
    def _():
        z_ref[...] = jnp.dot(hn, wz_ref[...], preferred_element_type=F32)


def _in_proj(x2, g, w_in_b, w_z, l, tm, tn):
    T = x2.shape[0]
    n_rows = T // tm
    z_index = lambda j, i: (jnp.where(j == 0, i, n_rows - 1), 0)
    return pl.pallas_call(
        _in_proj_kernel,
        grid=(MAIN_WIDTH // tn, n_rows),
        in_specs=[
            pl.BlockSpec((tm, D_MODEL), lambda j, i: (i, 0)),
            pl.BlockSpec((None, 1, D_MODEL), lambda j, i: (l, 0, 0)),
            pl.BlockSpec((None, D_MODEL, tn), lambda j, i: (l, 0, j)),
            pl.BlockSpec((None, D_MODEL, LANES), lambda j, i: (l, 0, 0)),
        ],
        out_specs=[
            pl.BlockSpec((tm, tn), lambda j, i: (i, j)),
            pl.BlockSpec((tm, LANES), z_index),
        ],
        out_shape=[
            jax.ShapeDtypeStruct((T, MAIN_WIDTH), BF16),
            jax.ShapeDtypeStruct((T, LANES), F32),
        ],
        compiler_params=_params(2),
        name="in_proj",
    )(x2, g, w_in_b, w_z)


def _head_meansq(xt):
    r = lax.broadcasted_iota(jnp.int32, (LANES, LANES), 0) // HEAD_DIM
    c = lax.broadcasted_iota(jnp.int32, (LANES, LANES), 1) // HEAD_DIM
    mean_blockdiag = jnp.where(r == c, 1.0 / HEAD_DIM, 0.0).astype(BF16)
    return jnp.dot((xt * xt).astype(BF16), mean_blockdiag, preferred_element_type=F32)


def _head_rmsnorm(xt, g):
    return xt * lax.rsqrt(_head_meansq(xt) + EPS) * g


def _swa_kernel(sink_ref, q_ref, k_ref, v_ref, gq_ref, gk_ref, wu_ref, wd_ref,
                o_ref, wub_ref, wdb_ref,
                qn_ref, kk_ref, vv_ref, tbl_ref, s0_ref, s1_ref, *, rows):
    b = pl.program_id(0)
    n = pl.program_id(1)
    blk = ATTN_BLOCK
    low = lax.broadcasted_iota(jnp.int32, (1, LANES), 1) < HEAD_DIM

    @pl.when(jnp.logical_and(b == 0, n == 0))
    def _():
        qi = lax.broadcasted_iota(jnp.int32, (blk, 2 * blk), 0)
        kj = lax.broadcasted_iota(jnp.int32, (blk, 2 * blk), 1)
        dist = qi + blk - kj
        valid = jnp.logical_and(dist >= 0, dist < blk)
        distf = dist.astype(F32)
        for qh in range(N_Q_HEADS):
            tbl_ref[qh] = jnp.where(valid, (-ALIBI_SLOPES[qh] * LOG2E) * distf, NEG)

    @pl.when(n == 0)
    def _():
        kk_ref[:, 0:blk, :] = jnp.zeros((2 * N_KV_HEADS, blk, LANES), BF16)
        vv_ref[:, 0:blk, :] = jnp.zeros((2 * N_KV_HEADS, blk, LANES), BF16)

    for t in range(ATTN_WIDTH // LANES):
        qt = q_ref[:, t * LANES:(t + 1) * LANES].astype(F32)
        qn = _head_rmsnorm(qt, gq_ref[...])
        qn_ref[:, t * LANES:(t + 1) * LANES] = qn.astype(BF16)

    for p in range(KV_WIDTH // LANES):
        kt = _head_rmsnorm(k_ref[:, p * LANES:(p + 1) * LANES].astype(F32), gk_ref[...])
        vt = v_ref[:, p * LANES:(p + 1) * LANES].astype(F32)
        k_even = jnp.where(low, kt, 0.0)
        k_odd = jnp.where(low, 0.0, kt)
        v_even = jnp.where(low, vt, 1.0)
        v_odd = jnp.where(low, 1.0, vt)
        h0, h1 = 2 * p, 2 * p + 1
        kk_ref[2 * h0 + 0, blk:, :] = k_even.astype(BF16)
        kk_ref[2 * h0 + 1, blk:, :] = pltpu.roll(k_even, HEAD_DIM, 1).astype(BF16)
        kk_ref[2 * h1 + 1, blk:, :] = k_odd.astype(BF16)
        kk_ref[2 * h1 + 0, blk:, :] = pltpu.roll(k_odd, HEAD_DIM, 1).astype(BF16)
        vv_ref[2 * h0 + 0, blk:, :] = v_even.astype(BF16)
        vv_ref[2 * h0 + 1, blk:, :] = pltpu.roll(v_even, HEAD_DIM, 1).astype(BF16)
        vv_ref[2 * h1 + 1, blk:, :] = v_odd.astype(BF16)
        vv_ref[2 * h1 + 0, blk:, :] = pltpu.roll(v_odd, HEAD_DIM, 1).astype(BF16)

    prev_cols = lax.broadcasted_iota(jnp.int32, (1, 2 * blk), 1) < blk

    def stage(nb_a, nb_b):
        for h in range(N_KV_HEADS):
            for tt in range(2):
                t = 2 * h + tt
                if nb_a is not None:
                    r0 = nb_a * blk
                    s_out = s_refs[nb_a % 2]
                    qs = qn_ref[r0:r0 + blk, t * LANES:(t + 1) * LANES]
                    for a in range(2):
                        qh = 4 * h + 2 * tt + a
                        keys = kk_ref[2 * h + a, r0:r0 + 2 * blk, :]
                        s_out[qh] = lax.dot_general(qs, keys, (((1,), (1,)), ((), ())),
                                                    preferred_element_type=F32)
                if nb_b is not None:
                    r0 = nb_b * blk
                    s_in = s_refs[nb_b % 2]
                    res, sink_terms = [], []
                    for a in range(2):
                        qh = 4 * h + 2 * tt + a
                        s = s_in[qh] + tbl_ref[qh]
                        if nb_b == 0:
                            s = s + negrow
                        sink = sink_ref[qh] * LOG2E
                        m = jnp.maximum(jnp.max(s, axis=-1, keepdims=True), sink)
                        p = jnp.exp2(s - m).astype(BF16)
                        vals = vv_ref[2 * h + a, r0:r0 + 2 * blk, :]
                        res.append(jnp.dot(p, vals, preferred_element_type=F32))
                        sink_terms.append(jnp.exp2(sink - m))
                    num = jnp.where(low, res[0], res[1])
                    den = pltpu.roll(jnp.where(low, res[1], res[0]), HEAD_DIM, 1)
                    den = den + jnp.where(low, sink_terms[0], sink_terms[1])
                    o_ref[r0:r0 + blk, t * LANES:(t + 1) * LANES] = (num / den).astype(o_ref.dtype)

    s_refs = (s0_ref, s1_ref)
    negrow = jnp.where(jnp.logical_and(n == 0, prev_cols), NEG, 0.0).astype(F32)
    nblk = rows // blk
    stage(0, None)
    for nb in range(nblk - 1):
        stage(nb + 1, nb)
    stage(None, nblk - 1)

    kk_ref[:, 0:blk, :] = kk_ref[:, rows:rows + blk, :]
    vv_ref[:, 0:blk, :] = vv_ref[:, rows:rows + blk, :]

    wub_ref[...] = wu_ref[...].astype(BF16)
    wdb_ref[...] = wd_ref[...].astype(BF16)


def _swa(proj, sinks, gq, gk, w_up, w_down, l, B, S, rows):
    T = B * S
    nsteps = S // rows
    total_steps = B * nsteps
    up_rows = D_MODEL // total_steps
    down_rows = D_FF // total_steps
    assert up_rows % 16 == 0 and up_rows * total_steps == D_MODEL and down_rows * total_steps == D_FF
    qcol = OFF_AQ // ATTN_WIDTH
    kcol = OFF_AK // KV_WIDTH
    vcol = OFF_AV // KV_WIDTH
    return pl.pallas_call(
        functools.partial(_swa_kernel, rows=rows),
        grid=(B, nsteps),
        in_specs=[
            pl.BlockSpec(memory_space=pltpu.SMEM),
            pl.BlockSpec((rows, ATTN_WIDTH), lambda b, n: (b * nsteps + n, qcol)),
            pl.BlockSpec((rows, KV_WIDTH), lambda b, n: (b * nsteps + n, kcol)),
            pl.BlockSpec((rows, KV_WIDTH), lambda b, n: (b * nsteps + n, vcol)),
            pl.BlockSpec((1, LANES), lambda b, n: (0, 0)),
            pl.BlockSpec((1, LANES), lambda b, n: (0, 0)),
            pl.BlockSpec((None, up_rows, D_FF), lambda b, n: (l, b * nsteps + n, 0)),
            pl.BlockSpec((None, down_rows, D_MODEL), lambda b, n: (l, b * nsteps + n, 0)),
        ],
        out_specs=[
            pl.BlockSpec((rows, ATTN_WIDTH), lambda b, n: (b * nsteps + n, 0)),
            pl.BlockSpec((up_rows, D_FF), lambda b, n: (b * nsteps + n, 0)),
            pl.BlockSpec((down_rows, D_MODEL), lambda b, n: (b * nsteps + n, 0)),
        ],
        out_shape=[
            jax.ShapeDtypeStruct((T, ATTN_WIDTH), BF16),
            jax.ShapeDtypeStruct((D_MODEL, D_FF), BF16),
            jax.ShapeDtypeStruct((D_FF, D_MODEL), BF16),
        ],
        scratch_shapes=[
            pltpu.VMEM((rows, ATTN_WIDTH), BF16),
            pltpu.VMEM((2 * N_KV_HEADS, rows + ATTN_BLOCK, LANES), BF16),
            pltpu.VMEM((2 * N_KV_HEADS, rows + ATTN_BLOCK, LANES), BF16),
            pltpu.VMEM((N_Q_HEADS, ATTN_BLOCK, 2 * ATTN_BLOCK), F32),
            pltpu.VMEM((N_Q_HEADS, ATTN_BLOCK, 2 * ATTN_BLOCK), F32),
            pltpu.VMEM((N_Q_HEADS, ATTN_BLOCK, 2 * ATTN_BLOCK), F32),
        ],
        compiler_params=_params(2),
        name="swa",
    )(sinks, proj, proj, proj, gq, gk, w_up, w_down)


def _log_sigmoid(x):
    return jnp.minimum(x, 0.0) - jnp.log(1.0 + jnp.exp(jnp.minimum(x, -x)))


def _silu(x):
    h = 0.5 * x
    return h * (1.0 + jnp.tanh(h))


def _gla_body(q_ref, k_ref, v_ref, r_ref, z_ref, gw_ref, gb_ref, ng_ref, o_ref, s_ref, la_ref, *, rows, c,
              after_unit):
    pair = 2 * GLA_CHUNK

    @pl.when(c == 0)
    def _():
        s_ref[...] = jnp.zeros_like(s_ref)

    ri = lax.broadcasted_iota(jnp.int32, (pair, pair), 0)
    ci = lax.broadcasted_iota(jnp.int32, (pair, pair), 1)
    causal = jnp.logical_and(ri // GLA_CHUNK == ci // GLA_CHUNK, ci <= ri)
    tril = causal.astype(BF16)
    low = lax.broadcasted_iota(jnp.int32, (1, pair), 1) < GLA_CHUNK

    logit = jnp.dot(z_ref[...].astype(BF16), gw_ref[...], preferred_element_type=F32) + gb_ref[...]
    la_ref[...] = _log_sigmoid(logit) * (1.0 / GLA_TAU)

    for pr in range(rows // pair):
        rs = slice(pr * pair, (pr + 1) * pair)
        la = la_ref[rs, :]
        la_hi = la.astype(BF16)
        la_lo = (la - la_hi.astype(F32)).astype(BF16)
        bcum = (jnp.dot(tril, la_hi, preferred_element_type=F32)
                + jnp.dot(tril, la_lo, preferred_element_type=F32))
        q = q_ref[rs, :].astype(F32) * (GLA_DK ** -0.5)
        k = k_ref[rs, :].astype(F32)
        q_in = (q * jnp.exp(bcum)).astype(BF16)
        k_in = (k * jnp.exp(-bcum)).astype(BF16)

        for h in range(GLA_HEADS):
            kc = slice(h * GLA_DK, (h + 1) * GLA_DK)
            vc = slice(h * GLA_DV, (h + 1) * GLA_DV)
            v = v_ref[rs, vc]
            qh = q_in[:, kc]
            att = lax.dot_general(qh, k_in[:, kc], (((1,), (1,)), ((), ())),
                                  preferred_element_type=F32)
            att = jnp.where(causal, att, 0.0).astype(BF16)
            o_intra = jnp.dot(att, v, preferred_element_type=F32)

            bt = bcum[:, kc].T
            kt = k[:, kc].T
            last_a = bt[:, GLA_CHUNK - 1:GLA_CHUNK]
            last_b = bt[:, pair - 1:pair]
            kst = kt * jnp.exp(jnp.where(low, last_a, last_b) - bt)
            kst_a = jnp.where(low, kst, 0.0).astype(BF16)
            kst_b = jnp.where(low, 0.0, kst).astype(BF16)

            state = s_ref[h]
            o_a = jnp.dot(qh[0:GLA_CHUNK], state.astype(BF16), preferred_element_type=F32)
            state = state * jnp.exp(last_a) + jnp.dot(kst_a, v, preferred_element_type=F32)
            o_b = jnp.dot(qh[GLA_CHUNK:pair], state.astype(BF16), preferred_element_type=F32)
            state = state * jnp.exp(last_b) + jnp.dot(kst_b, v, preferred_element_type=F32)
            s_ref[h] = state

            o = o_intra + jnp.concatenate([o_a, o_b], axis=0)
            y = _rmsnorm_rows(o, ng_ref[...])
            gate = _silu(r_ref[rs, vc].astype(F32))
            o_ref[rs, vc] = (y * gate).astype(o_ref.dtype)
            after_unit(pr * GLA_HEADS + h)


class _TwoHalves:
    def __init__(self, lo, hi, half):
        self.lo, self.hi, self.half = lo, hi, half

    def __getitem__(self, idx):
        rows, cols = idx
        if cols.start >= self.half:
            return self.hi[rows, cols.start - self.half:cols.stop - self.half]
        assert cols.stop <= self.half
        return self.lo[rows, cols]


def _gla_out_kernel(q_ref, k_ref, v0_ref, v1_ref, r0_ref, r1_ref, z_ref, gw_ref, gb_ref, ng_ref,
                    x_ref, a_ref, w_ref, xo_ref, s_ref, la_ref, og_ref, og_prev_ref, *, rows, nsteps):
    t = pl.program_id(0)

    @pl.when(t == 0)
    def _():
        og_prev_ref[...] = jnp.zeros_like(og_prev_ref)

    n_units = (rows // (2 * GLA_CHUNK)) * GLA_HEADS
    n_slices = D_MODEL // OUT_SLICE
    assert n_units == 2 * n_slices

    def project_slice(unit):
        cs = slice((unit // 2) * OUT_SLICE, (unit // 2 + 1) * OUT_SLICE)
        if unit % 2 == 0:
            xo_ref[:, cs] = x_ref[:, cs] + jnp.dot(a_ref[...], w_ref[0:ATTN_WIDTH, cs],
                                                   preferred_element_type=F32)
        else:
            xo_ref[:, cs] += jnp.dot(og_prev_ref[...], w_ref[ATTN_WIDTH:, cs], preferred_element_type=F32)

    half = GLA_WIDTH // 2
    _gla_body(q_ref, k_ref, _TwoHalves(v0_ref, v1_ref, half), _TwoHalves(r0_ref, r1_ref, half),
              z_ref, gw_ref, gb_ref, ng_ref, og_ref, s_ref, la_ref, rows=rows, c=lax.rem(t, nsteps),
              after_unit=project_slice)
    og_prev_ref[...] = og_ref[...]


def _gla_out(proj, z, gate_w, gate_b, norm_g, x2, o_attn, w_out, l, B, S, rows):
    T = B * S
    nsteps = S // rows
    ntiles = B * nsteps
    cq = OFF_GQ // GLA_QK_WIDTH
    ck = OFF_GK // GLA_QK_WIDTH
    assert OFF_GV % 512 == 0 and OFF_GR % 512 == 0
    cur = lambda t: jnp.minimum(t, ntiles - 1)
    prev = lambda t: jnp.maximum(t - 1, 0)
    half = GLA_WIDTH // 2
    return pl.pallas_call(
        functools.partial(_gla_out_kernel, rows=rows, nsteps=nsteps),
        grid=(ntiles + 1,),
        in_specs=[
            pl.BlockSpec((rows, GLA_QK_WIDTH), lambda t: (cur(t), cq)),
            pl.BlockSpec((rows, GLA_QK_WIDTH), lambda t: (cur(t), ck)),
            pl.BlockSpec((rows, half), lambda t: (cur(t), OFF_GV // half)),
            pl.BlockSpec((rows, half), lambda t: (cur(t), OFF_GV // half + 1)),
            pl.BlockSpec((rows, half), lambda t: (cur(t), OFF_GR // half)),
            pl.BlockSpec((rows, half), lambda t: (cur(t), OFF_GR // half + 1)),
            pl.BlockSpec((rows, LANES), lambda t: (cur(t), 0)),
            pl.BlockSpec((None, LANES, GLA_QK_WIDTH), lambda t: (l, 0, 0)),
            pl.BlockSpec((None, 1, GLA_QK_WIDTH), lambda t: (l, 0, 0)),
            pl.BlockSpec((None, 1, GLA_DV), lambda t: (l, 0, 0)),
            pl.BlockSpec((rows, D_MODEL), lambda t: (prev(t), 0)),
            pl.BlockSpec((rows, ATTN_WIDTH), lambda t: (prev(t), 0)),
            pl.BlockSpec((None, ATTN_WIDTH + GLA_WIDTH, D_MODEL), lambda t: (l, 0, 0)),
        ],
        out_specs=pl.BlockSpec((rows, D_MODEL), lambda t: (prev(t), 0)),
        out_shape=jax.ShapeDtypeStruct((T, D_MODEL), F32),
        scratch_shapes=[
            pltpu.VMEM((GLA_HEADS, GLA_DK, GLA_DV), F32),
            pltpu.VMEM((rows, GLA_QK_WIDTH), F32),
            pltpu.VMEM((rows, GLA_WIDTH), BF16),
            pltpu.VMEM((rows, GLA_WIDTH), BF16),
        ],
        compiler_params=_params(1),
        name="gla_out",
    )(proj, proj, proj, proj, proj, proj, z, gate_w, gate_b, norm_g, x2, o_attn, w_out)


def _ffn_kernel(x_ref, g_ref, wu_ref, wd_ref, o_ref, hn_ref, *, row_chunk):
    f = pl.program_id(1)

    @pl.when(f == 0)
    def _():
        def body(r, carry):
            r0 = pl.multiple_of(r * row_chunk, row_chunk)
            xr = x_ref[pl.ds(r0, row_chunk), :]
            hn_ref[pl.ds(r0, row_chunk), :] = _rmsnorm_rows(xr, g_ref[...]).astype(BF16)
            o_ref[pl.ds(r0, row_chunk), :] = xr
            return carry
        lax.fori_loop(0, x_ref.shape[0] // row_chunk, body, 0)

    u = jnp.dot(hn_ref[...], wu_ref[...], preferred_element_type=F32)
    u = jnp.maximum(u, 0.0)
    u = (u * u).astype(BF16)
    o_ref[...] += jnp.dot(u, wd_ref[...], preferred_element_type=F32)


def _ffn(x2, g, w_up_b, w_down_b, l, tm, tf):
    T = x2.shape[0]
    return pl.pallas_call(
        functools.partial(_ffn_kernel, row_chunk=min(128, tm)),
        grid=(T // tm, D_FF // tf),
        in_specs=[
            pl.BlockSpec((tm, D_MODEL), lambda i, f: (i, 0)),
            pl.BlockSpec((None, 1, D_MODEL), lambda i, f: (l, 0, 0)),
            pl.BlockSpec((D_MODEL, tf), lambda i, f: (0, f)),
            pl.BlockSpec((tf, D_MODEL), lambda i, f: (f, 0)),
        ],
        out_specs=pl.BlockSpec((tm, D_MODEL), lambda i, f: (i, 0)),
        out_shape=jax.ShapeDtypeStruct((T, D_MODEL), F32),
        scratch_shapes=[pltpu.VMEM((tm, D_MODEL), BF16)],
        compiler_params=_params(2),
        name="ffn",
    )(x2, g, w_up_b, w_down_b)


def _tile_plan(T, S):
    tm = min(1024, T)
    mix_rows = min(512, S)
    assert T % tm == 0 and S % mix_rows == 0 and mix_rows % (2 * GLA_CHUNK) == 0
    return dict(tm=tm, tn_in=2304, tf=1024, mix_rows=mix_rows)


def kernel(x, norm1_g, w_in, q_norm_g, k_norm_g, attn_sinks, gla_gate_w, gla_gate_b, gla_norm_g,
           w_out, norm2_g, w_up, w_down):
    B, S, D = x.shape
    assert D == D_MODEL
    T = B * S
    plan = _tile_plan(T, S)
    depth = w_in.shape[0]
    x2 = x.reshape(T, D)

    w_in_b = w_in.astype(BF16)
    w_z = jnp.pad(w_in[:, :, MAIN_WIDTH:], ((0, 0), (0, 0), (0, LANES - GLA_RANK))).astype(BF16)
    gate_w = jnp.pad(gla_gate_w, ((0, 0), (0, LANES - GLA_RANK), (0, 0))).astype(BF16)
    norm1 = norm1_g.reshape(depth, 1, D)
    norm2 = norm2_g.reshape(depth, 1, D)
    gate_b = gla_gate_b.reshape(depth, 1, GLA_QK_WIDTH)
    gla_g = gla_norm_g.reshape(depth, 1, GLA_DV)

    for l in range(depth):
        gq = jnp.tile(q_norm_g[l] * (HEAD_DIM ** -0.5 * LOG2E), 2).reshape(1, LANES)
        gk = jnp.tile(k_norm_g[l], 2).reshape(1, LANES)

        proj, z = _in_proj(x2, norm1, w_in_b, w_z, l, plan["tm"], plan["tn_in"])
        o_attn, w_up_b, w_down_b = _swa(proj, attn_sinks[l], gq, gk, w_up, w_down, l, B, S, plan["mix_rows"])
        x2 = _gla_out(proj, z, gate_w, gate_b, gla_g, x2, o_attn, w_out, l, B, S, plan["mix_rows"])
        x2 = _ffn(x2, norm2, w_up_b, w_down_b, l, plan["tm"], plan["tf"])
    return x2.reshape(B, S, D)
```

```python
import functools

import jax
import jax.numpy as jnp
from jax import lax
from jax.experimental import pallas as pl
from jax.experimental.pallas import tpu as pltpu

F32 = jnp.float32
BF16 = jnp.bfloat16

D_MODEL = 2048
HEAD_DIM = 64
N_Q_HEADS = 16
N_KV_HEADS = 4
ATTN_BLOCK = 128
ATTN_WIDTH = N_Q_HEADS * HEAD_DIM
KV_WIDTH = N_KV_HEADS * HEAD_DIM
GLA_HEADS = 4
GLA_DK = 128
GLA_DV = 256
GLA_QK_WIDTH = GLA_HEADS * GLA_DK
GLA_WIDTH = GLA_HEADS * GLA_DV
GLA_RANK = 16
GLA_TAU = 16.0
GLA_CHUNK = 64
D_FF = 4 * D_MODEL
EPS = 1e-6
NEG = -1e30
LOG2E = 1.4426950408889634

LANES = 128
OUT_SLICE = 256
MAIN_WIDTH = 4608
OFF_AQ, OFF_AK, OFF_AV = 0, 1024, 1280
OFF_GQ, OFF_GK, OFF_GV, OFF_GR = 1536, 2048, 2560, 3584

VMEM_LIMIT = 60 * 1024 * 1024

ALIBI_SLOPES =tuple(2.0 ** (-8.0 * (i + 1) / N_Q_HEADS) for i in range(N_Q_HEADS))


def _params(n_axes, flags=None):
    return pltpu.CompilerParams(dimension_semantics=("arbitrary",) * n_axes,
                                vmem_limit_bytes=VMEM_LIMIT, flags=flags)


def _rmsnorm_rows(x, g):
    ms = jnp.mean(x * x, axis=-1, keepdims=True)
    return (x * lax.rsqrt(ms + EPS)) * g


def _in_proj_kernel(x_ref, g_ref, w_ref, wz_ref, o_ref, z_ref):
    hn = _rmsnorm_rows(x_ref[...], g_ref[...]).astype(BF16)
    o_ref[...] = jnp.dot(hn, w_ref[...], preferred_element_type=F32).astype(o_ref.dtype)

    @pl.when(pl.program_id(0) == 0)
    def _():
        z_ref[...] = jnp.dot(hn, wz_ref[...], preferred_element_type=F32)


def _in_proj(x2, g, w_in_b, w_z, l, tm, tn):
    T = x2.shape[0]
    n_rows = T // tm
    z_index = lambda j, i: (jnp.where(j == 0, i, n_rows - 1), 0)
    return pl.pallas_call(
        _in_proj_kernel,
        grid=(MAIN_WIDTH // tn, n_rows),
        in_specs=[
            pl.BlockSpec((tm, D_MODEL), lambda j, i: (i, 0)),
            pl.BlockSpec((None, 1, D_MODEL), lambda j, i: (l, 0, 0)),
            pl.BlockSpec((None, D_MODEL, tn), lambda j, i: (l, 0, j)),
            pl.BlockSpec((None, D_MODEL, LANES), lambda j, i: (l, 0, 0)),
        ],
        out_specs=[
            pl.BlockSpec((tm, tn), lambda j, i: (i, j)),
            pl.BlockSpec((tm, LANES), z_index),
        ],
        out_shape=[
            jax.ShapeDtypeStruct((T, MAIN_WIDTH), BF16),
            jax.ShapeDtypeStruct((T, LANES), F32),
        ],
        compiler_params=_params(2),
        name="in_proj",
    )(x2, g, w_in_b, w_z)


def _head_meansq(xt):
    r = lax.broadcasted_iota(jnp.int32, (LANES, LANES), 0) // HEAD_DIM
    c = lax.broadcasted_iota(jnp.int32, (LANES, LANES), 1) // HEAD_DIM
    mean_blockdiag = jnp.where(r == c, 1.0 / HEAD_DIM, 0.0).astype(BF16)
    return jnp.dot((xt * xt).astype(BF16), mean_blockdiag, preferred_element_type=F32)


def _head_rmsnorm(xt, g):
    return xt * lax.rsqrt(_head_meansq(xt) + EPS) * g


def _swa_kernel(sink_ref, q_ref, k_ref, v_ref, gq_ref, gk_ref, wu_ref, wd_ref,
                o_ref, wub_ref, wdb_ref,
                qn_ref, kk_ref, vv_ref, tbl_ref, *, rows):
    b = pl.program_id(0)
    n = pl.program_id(1)
    blk = ATTN_BLOCK
    low = lax.broadcasted_iota(jnp.int32, (1, LANES), 1) < HEAD_DIM

    @pl.when(jnp.logical_and(b == 0, n == 0))
    def _():
        qi = lax.broadcasted_iota(jnp.int32, (blk, 2 * blk), 0)
        kj = lax.broadcasted_iota(jnp.int32, (blk, 2 * blk), 1)
        dist = qi + blk - kj
        valid = jnp.logical_and(dist >= 0, dist < blk)
        distf = dist.astype(F32)
        for qh in range(N_Q_HEADS):
            tbl_ref[qh] = jnp.where(valid, (-ALIBI_SLOPES[qh] * LOG2E) * distf, NEG)

    @pl.when(n == 0)
    def _():
        kk_ref[:, 0:blk, :] = jnp.zeros((2 * N_KV_HEADS, blk, LANES), BF16)
        vv_ref[:, 0:blk, :] = jnp.zeros((2 * N_KV_HEADS, blk, LANES), BF16)

    for t in range(ATTN_WIDTH // LANES):
        qt = q_ref[:, t * LANES:(t + 1) * LANES].astype(F32)
        qn = _head_rmsnorm(qt, gq_ref[...])
        qn_ref[:, t * LANES:(t + 1) * LANES] = qn.astype(BF16)

    for p in range(KV_WIDTH // LANES):
        kt = _head_rmsnorm(k_ref[:, p * LANES:(p + 1) * LANES].astype(F32), gk_ref[...])
        vt = v_ref[:, p * LANES:(p + 1) * LANES].astype(F32)
        k_even = jnp.where(low, kt, 0.0)
        k_odd = jnp.where(low, 0.0, kt)
        v_even = jnp.where(low, vt, 1.0)
        v_odd = jnp.where(low, 1.0, vt)
        h0, h1 = 2 * p, 2 * p + 1
        kk_ref[2 * h0 + 0, blk:, :] = k_even.astype(BF16)
        kk_ref[2 * h0 + 1, blk:, :] = pltpu.roll(k_even, HEAD_DIM, 1).astype(BF16)
        kk_ref[2 * h1 + 1, blk:, :] = k_odd.astype(BF16)
        kk_ref[2 * h1 + 0, blk:, :] = pltpu.roll(k_odd, HEAD_DIM, 1).astype(BF16)
        vv_ref[2 * h0 + 0, blk:, :] = v_even.astype(BF16)
        vv_ref[2 * h0 + 1, blk:, :] = pltpu.roll(v_even, HEAD_DIM, 1).astype(BF16)
        vv_ref[2 * h1 + 1, blk:, :] = v_odd.astype(BF16)
        vv_ref[2 * h1 + 0, blk:, :] = pltpu.roll(v_odd, HEAD_DIM, 1).astype(BF16)

    prev_cols = lax.broadcasted_iota(jnp.int32, (1, 2 * blk), 1) < blk

    negrow = jnp.where(jnp.logical_and(n == 0, prev_cols), NEG, 0.0).astype(F32)
    for nb in range(rows // blk):
        r0 = nb * blk
        for t in range(ATTN_WIDTH // LANES):
            h, tt = divmod(t, 2)
            qs = qn_ref[r0:r0 + blk, t * LANES:(t + 1) * LANES]
            res, sink_terms = [], []
            for a in range(2):
                qh = 4 * h + 2 * tt + a
                keys = kk_ref[2 * h + a, r0:r0 + 2 * blk, :]
                s = lax.dot_general(qs, keys, (((1,), (1,)), ((), ())), preferred_element_type=F32)
                s = s + tbl_ref[qh]
                if nb == 0:
                    s = s + negrow
                sink = sink_ref[qh] * LOG2E
                m = jnp.maximum(jnp.max(s, axis=-1, keepdims=True), sink)
                p = jnp.exp2(s - m).astype(BF16)
                vals = vv_ref[2 * h + a, r0:r0 + 2 * blk, :]
                res.append(jnp.dot(p, vals, preferred_element_type=F32))
                sink_terms.append(jnp.exp2(sink - m))
            num = jnp.where(low, res[0], res[1])
            den = pltpu.roll(jnp.where(low, res[1], res[0]), HEAD_DIM, 1)
            den = den + jnp.where(low, sink_terms[0], sink_terms[1])
            o_ref[r0:r0 + blk, t * LANES:(t + 1) * LANES] = (num / den).astype(o_ref.dtype)

    kk_ref[:, 0:blk, :] = kk_ref[:, rows:rows + blk, :]
    vv_ref[:, 0:blk, :] = vv_ref[:, rows:rows + blk, :]

    wub_ref[...] = wu_ref[...].astype(BF16)
    wdb_ref[...] = wd_ref[...].astype(BF16)


def _swa(proj, sinks, gq, gk, w_up, w_down, l, B, S, rows):
    T = B * S
    nsteps = S // rows
    total_steps = B * nsteps
    up_rows = D_MODEL // total_steps
    down_rows = D_FF // total_steps
    assert up_rows % 16 == 0 and up_rows * total_steps == D_MODEL and down_rows * total_steps == D_FF
    qcol = OFF_AQ // ATTN_WIDTH
    kcol = OFF_AK // KV_WIDTH
    vcol = OFF_AV // KV_WIDTH
    return pl.pallas_call(
        functools.partial(_swa_kernel, rows=rows),
        grid=(B, nsteps),
        in_specs=[
            pl.BlockSpec(memory_space=pltpu.SMEM),
            pl.BlockSpec((rows, ATTN_WIDTH), lambda b, n: (b * nsteps + n, qcol)),
            pl.BlockSpec((rows, KV_WIDTH), lambda b, n: (b * nsteps + n, kcol)),
            pl.BlockSpec((rows, KV_WIDTH), lambda b, n: (b * nsteps + n, vcol)),
            pl.BlockSpec((1, LANES), lambda b, n: (0, 0)),
            pl.BlockSpec((1, LANES), lambda b, n: (0, 0)),
            pl.BlockSpec((None, up_rows, D_FF), lambda b, n: (l, b * nsteps + n, 0)),
            pl.BlockSpec((None, down_rows, D_MODEL), lambda b, n: (l, b * nsteps + n, 0)),
        ],
        out_specs=[
            pl.BlockSpec((rows, ATTN_WIDTH), lambda b, n: (b * nsteps + n, 0)),
            pl.BlockSpec((up_rows, D_FF), lambda b, n: (b * nsteps + n, 0)),
            pl.BlockSpec((down_rows, D_MODEL), lambda b, n: (b * nsteps + n, 0)),
        ],
        out_shape=[
            jax.ShapeDtypeStruct((T, ATTN_WIDTH), BF16),
            jax.ShapeDtypeStruct((D_MODEL, D_FF), BF16),
            jax.ShapeDtypeStruct((D_FF, D_MODEL), BF16),
        ],
        scratch_shapes=[
            pltpu.VMEM((rows, ATTN_WIDTH), BF16),
            pltpu.VMEM((2 * N_KV_HEADS, rows + ATTN_BLOCK, LANES), BF16),
            pltpu.VMEM((2 * N_KV_HEADS, rows + ATTN_BLOCK, LANES), BF16),
            pltpu.VMEM((N_Q_HEADS, ATTN_BLOCK, 2 * ATTN_BLOCK), F32),
        ],
        compiler_params=_params(2),
        name="swa",
    )(sinks, proj, proj, proj, gq, gk, w_up, w_down)


def _log_sigmoid(x):
    return jnp.minimum(x, 0.0) - jnp.log(1.0 + jnp.exp(jnp.minimum(x, -x)))


def _silu(x):
    h = 0.5 * x
    return h * (1.0 + jnp.tanh(h))


def _gla_body(q_ref, k_ref, v_ref, r_ref, z_ref, gw_ref, gb_ref, ng_ref, o_ref, s_ref, la_ref, *, rows, c,
              after_unit):
    pair = 2 * GLA_CHUNK

    @pl.when(c == 0)
    def _():
        s_ref[...] = jnp.zeros_like(s_ref)

    ri = lax.broadcasted_iota(jnp.int32, (pair, pair), 0)
    ci = lax.broadcasted_iota(jnp.int32, (pair, pair), 1)
    causal = jnp.logical_and(ri // GLA_CHUNK == ci // GLA_CHUNK, ci <= ri)
    tril = causal.astype(BF16)
    low = lax.broadcasted_iota(jnp.int32, (1, pair), 1) < GLA_CHUNK

    logit = jnp.dot(z_ref[...].astype(BF16), gw_ref[...], preferred_element_type=F32) + gb_ref[...]
    la_ref[...] = _log_sigmoid(logit) * (1.0 / GLA_TAU)

    for pr in range(rows // pair):
        rs = slice(pr * pair, (pr + 1) * pair)
        la = la_ref[rs, :]
        la_hi = la.astype(BF16)
        la_lo = (la - la_hi.astype(F32)).astype(BF16)
        bcum = (jnp.dot(tril, la_hi, preferred_element_type=F32)
                + jnp.dot(tril, la_lo, preferred_element_type=F32))
        q = q_ref[rs, :].astype(F32) * (GLA_DK ** -0.5)
        k = k_ref[rs, :].astype(F32)
        q_in = (q * jnp.exp(bcum)).astype(BF16)
        k_in = (k * jnp.exp(-bcum)).astype(BF16)

        for h in range(GLA_HEADS):
            kc = slice(h * GLA_DK, (h + 1) * GLA_DK)
            vc = slice(h * GLA_DV, (h + 1) * GLA_DV)
            v = v_ref[rs, vc]
            qh = q_in[:, kc]
            att = lax.dot_general(qh, k_in[:, kc], (((1,), (1,)), ((), ())),
                                  preferred_element_type=F32)
            att = jnp.where(causal, att, 0.0).astype(BF16)
            o_intra = jnp.dot(att, v, preferred_element_type=F32)

            bt = bcum[:, kc].T
            kt = k[:, kc].T
            last_a = bt[:, GLA_CHUNK - 1:GLA_CHUNK]
            last_b = bt[:, pair - 1:pair]
            kst = kt * jnp.exp(jnp.where(low, last_a, last_b) - bt)
            kst_a = jnp.where(low, kst, 0.0).astype(BF16)
            kst_b = jnp.where(low, 0.0, kst).astype(BF16)

            state = s_ref[h]
            o_a = jnp.dot(qh[0:GLA_CHUNK], state.astype(BF16), preferred_element_type=F32)
            state = state * jnp.exp(last_a) + jnp.dot(kst_a, v, preferred_element_type=F32)
            o_b = jnp.dot(qh[GLA_CHUNK:pair], state.astype(BF16), preferred_element_type=F32)
            state = state * jnp.exp(last_b) + jnp.dot(kst_b, v, preferred_element_type=F32)
            s_ref[h] = state

            o = o_intra + jnp.concatenate([o_a, o_b], axis=0)
            y = _rmsnorm_rows(o, ng_ref[...])
            gate = _silu(r_ref[rs, vc].astype(F32))
            o_ref[rs, vc] = (y * gate).astype(o_ref.dtype)
            after_unit(pr * GLA_HEADS + h)


class _TwoHalves:
    def __init__(self, lo, hi, half):
        self.lo, self.hi, self.half = lo, hi, half

    def __getitem__(self, idx):
        rows, cols = idx
        if cols.start >= self.half:
            return self.hi[rows, cols.start - self.half:cols.stop - self.half]
        assert cols.stop <= self.half
        return self.lo[rows, cols]


def _gla_out_kernel(q_ref, k_ref, v0_ref, v1_ref, r0_ref, r1_ref, z_ref, gw_ref, gb_ref, ng_ref,
                    x_ref, a_ref, w_ref, xo_ref, s_ref, la_ref, og_ref, og_prev_ref, *, rows, nsteps):
    t = pl.program_id(0)

    @pl.when(t == 0)
    def _():
        og_prev_ref[...] = jnp.zeros_like(og_prev_ref)

    n_units = (rows // (2 * GLA_CHUNK)) * GLA_HEADS
    n_slices = D_MODEL // OUT_SLICE
    assert n_units == 2 * n_slices

    def project_slice(unit):
        cs = slice((unit // 2) * OUT_SLICE, (unit // 2 + 1) * OUT_SLICE)
        if unit % 2 == 0:
            xo_ref[:, cs] = x_ref[:, cs] + jnp.dot(a_ref[...], w_ref[0:ATTN_WIDTH, cs],
                                                   preferred_element_type=F32)
        else:
            xo_ref[:, cs] += jnp.dot(og_prev_ref[...], w_ref[ATTN_WIDTH:, cs], preferred_element_type=F32)

    half = GLA_WIDTH // 2
    _gla_body(q_ref, k_ref, _TwoHalves(v0_ref, v1_ref, half), _TwoHalves(r0_ref, r1_ref, half),
              z_ref, gw_ref, gb_ref, ng_ref, og_ref, s_ref, la_ref, rows=rows, c=lax.rem(t, nsteps),
              after_unit=project_slice)
    og_prev_ref[...] = og_ref[...]


def _gla_out(proj, z, gate_w, gate_b, norm_g, x2, o_attn, w_out, l, B, S, rows):
    T = B * S
    nsteps = S // rows
    ntiles = B * nsteps
    cq = OFF_GQ // GLA_QK_WIDTH
    ck = OFF_GK // GLA_QK_WIDTH
    assert OFF_GV % 512 == 0 and OFF_GR % 512 == 0
    cur = lambda t: jnp.minimum(t, ntiles - 1)
    prev = lambda t: jnp.maximum(t - 1, 0)
    half = GLA_WIDTH // 2
    return pl.pallas_call(
        functools.partial(_gla_out_kernel, rows=rows, nsteps=nsteps),
        grid=(ntiles + 1,),
        in_specs=[
            pl.BlockSpec((rows, GLA_QK_WIDTH), lambda t: (cur(t), cq)),
            pl.BlockSpec((rows, GLA_QK_WIDTH), lambda t: (cur(t), ck)),
            pl.BlockSpec((rows, half), lambda t: (cur(t), OFF_GV // half)),
            pl.BlockSpec((rows, half), lambda t: (cur(t), OFF_GV // half + 1)),
            pl.BlockSpec((rows, half), lambda t: (cur(t), OFF_GR // half)),
            pl.BlockSpec((rows, half), lambda t: (cur(t), OFF_GR // half + 1)),
            pl.BlockSpec((rows, LANES), lambda t: (cur(t), 0)),
            pl.BlockSpec((None, LANES, GLA_QK_WIDTH), lambda t: (l, 0, 0)),
            pl.BlockSpec((None, 1, GLA_QK_WIDTH), lambda t: (l, 0, 0)),
            pl.BlockSpec((None, 1, GLA_DV), lambda t: (l, 0, 0)),
            pl.BlockSpec((rows, D_MODEL), lambda t: (prev(t), 0)),
            pl.BlockSpec((rows, ATTN_WIDTH), lambda t: (prev(t), 0)),
            pl.BlockSpec((None, ATTN_WIDTH + GLA_WIDTH, D_MODEL), lambda t: (l, 0, 0)),
        ],
        out_specs=pl.BlockSpec((rows, D_MODEL), lambda t: (prev(t), 0)),
        out_shape=jax.ShapeDtypeStruct((T, D_MODEL), F32),
        scratch_shapes=[
            pltpu.VMEM((GLA_HEADS, GLA_DK, GLA_DV), F32),
            pltpu.VMEM((rows, GLA_QK_WIDTH), F32),
            pltpu.VMEM((rows, GLA_WIDTH), BF16),
            pltpu.VMEM((rows, GLA_WIDTH), BF16),
        ],
        compiler_params=_params(1),
        name="gla_out",
    )(proj, proj, proj, proj, proj, proj, z, gate_w, gate_b, norm_g, x2, o_attn, w_out)


def _ffn_kernel(x_ref, g_ref, wu_ref, wd_ref, o_ref, hn_ref):
    f = pl.program_id(1)

    @pl.when(f == 0)
    def _():
        x = x_ref[...]
        hn_ref[...] = _rmsnorm_rows(x, g_ref[...]).astype(BF16)
        o_ref[...] = x

    u = jnp.dot(hn_ref[...], wu_ref[...], preferred_element_type=F32)
    u = jnp.maximum(u, 0.0)
    u = (u * u).astype(BF16)
    o_ref[...] += jnp.dot(u, wd_ref[...], preferred_element_type=F32)


def _ffn(x2, g, w_up_b, w_down_b, l, tm, tf):
    T = x2.shape[0]
    return pl.pallas_call(
        _ffn_kernel,
        grid=(T // tm, D_FF // tf),
        in_specs=[
            pl.BlockSpec((tm, D_MODEL), lambda i, f: (i, 0)),
            pl.BlockSpec((None, 1, D_MODEL), lambda i, f: (l, 0, 0)),
            pl.BlockSpec((D_MODEL, tf), lambda i, f: (0, f)),
            pl.BlockSpec((tf, D_MODEL), lambda i, f: (f, 0)),
        ],
        out_specs=pl.BlockSpec((tm, D_MODEL), lambda i, f: (i, 0)),
        out_shape=jax.ShapeDtypeStruct((T, D_MODEL), F32),
        scratch_shapes=[pltpu.VMEM((tm, D_MODEL), BF16)],
        compiler_params=_params(2),
        name="ffn",
    )(x2, g, w_up_b, w_down_b)


def _tile_plan(T, S):
    tm = min(1024, T)
    mix_rows = min(512, S)
    assert T % tm == 0 and S % mix_rows == 0 and mix_rows % (2 * GLA_CHUNK) == 0
    return dict(tm=tm, tn_in=2304, tf=1024, mix_rows=mix_rows)


def kernel(x, norm1_g, w_in, q_norm_g, k_norm_g, attn_sinks, gla_gate_w, gla_gate_b, gla_norm_g,
           w_out, norm2_g, w_up, w_down):
    B, S, D = x.shape
    assert D == D_MODEL
    T = B * S
    plan = _tile_plan(T, S)
    depth = w_in.shape[0]
    x2 = x.reshape(T, D)

    w_in_b = w_in.astype(BF16)
    w_z = jnp.pad(w_in[:, :, MAIN_WIDTH:], ((0, 0), (0, 0), (0, LANES - GLA_RANK))).astype(BF16)
    gate_w = jnp.pad(gla_gate_w, ((0, 0), (0, LANES - GLA_RANK), (0, 0))).astype(BF16)
    norm1 = norm1_g.reshape(depth, 1, D)
    norm2 = norm2_g.reshape(depth, 1, D)
    gate_b = gla_gate_b.reshape(depth, 1, GLA_QK_WIDTH)
    gla_g = gla_norm_g.reshape(depth, 1, GLA_DV)

    for l in range(depth):
        gq = jnp.tile(q_norm_g[l] * (HEAD_DIM ** -0.5 * LOG2E), 2).reshape(1, LANES)
        gk = jnp.tile(k_norm_g[l], 2).reshape(1, LANES)

        proj, z = _in_proj(x2, norm1, w_in_b, w_z, l, plan["tm"], plan["tn_in"])
        o_attn, w_up_b, w_down_b = _swa(proj, attn_sinks[l], gq, gk, w_up, w_down, l, B, S, plan["mix_rows"])
        x2 = _gla_out(proj, z, gate_w, gate_b, gla_g, x2, o_attn, w_out, l, B, S, plan["mix_rows"])
        x2 = _ffn(x2, norm2, w_up_b, w_down_b, l, plan["tm"], plan["tf"])
    return x2.reshape(B, S, D)
```

```python
import functools

import jax
import jax.numpy as jnp
from jax import lax
from jax.experimental import pallas as pl
from jax.experimental.pallas import tpu as pltpu

F32 = jnp.float32
BF16 = jnp.bfloat16

D_MODEL = 2048
HEAD_DIM = 64
N_Q_HEADS = 16
N_KV_HEADS = 4
ATTN_BLOCK = 128
ATTN_WIDTH = N_Q_HEADS * HEAD_DIM
KV_WIDTH = N_KV_HEADS * HEAD_DIM
GLA_HEADS = 4
GLA_DK = 128
GLA_DV = 256
GLA_QK_WIDTH = GLA_HEADS * GLA_DK
GLA_WIDTH = GLA_HEADS * GLA_DV
GLA_RANK = 16
GLA_TAU = 16.0
GLA_CHUNK = 64
D_FF = 4 * D_MODEL
EPS = 1e-6
NEG = -1e30
LOG2E = 1.4426950408889634

LANES = 128
OUT_SLICE = 256
MAIN_WIDTH = 4608
OFF_AQ, OFF_AK, OFF_AV = 0, 1024, 1280
OFF_GQ, OFF_GK, OFF_GV, OFF_GR = 1536, 2048, 2560, 3584

VMEM_LIMIT = 60 * 1024 * 1024

ALIBI_SLOPES =tuple(2.0 ** (-8.0 * (i + 1) / N_Q_HEADS) for i in range(N_Q_HEADS))


def _params(n_axes, flags=None):
    return pltpu.CompilerParams(dimension_semantics=("arbitrary",) * n_axes,
                                vmem_limit_bytes=VMEM_LIMIT, flags=flags)


def _rmsnorm_rows(x, g):
    ms = jnp.mean(x * x, axis=-1, keepdims=True)
    return (x * lax.rsqrt(ms + EPS)) * g


def _in_proj_kernel(x_ref, g_ref, w_ref, wz_ref, o_ref, z_ref):
    hn = _rmsnorm_rows(x_ref[...], g_ref[...]).astype(BF16)
    o_ref[...] = jnp.dot(hn, w_ref[...], preferred_element_type=F32).astype(o_ref.dtype)

    @pl.when(pl.program_id(0) == 0)
    def _():
        z_ref[...] = jnp.dot(hn, wz_ref[...], preferred_element_type=F32)


def _in_proj(x2, g, w_in_b, w_z, l, tm, tn):
    T = x2.shape[0]
    n_rows = T // tm
    z_index = lambda j, i: (jnp.where(j == 0, i, n_rows - 1), 0)
    return pl.pallas_call(
        _in_proj_kernel,
        grid=(MAIN_WIDTH // tn, n_rows),
        in_specs=[
            pl.BlockSpec((tm, D_MODEL), lambda j, i: (i, 0)),
            pl.BlockSpec((None, 1, D_MODEL), lambda j, i: (l, 0, 0)),
            pl.BlockSpec((None, D_MODEL, tn), lambda j, i: (l, 0, j)),
            pl.BlockSpec((None, D_MODEL, LANES), lambda j, i: (l, 0, 0)),
        ],
        out_specs=[
            pl.BlockSpec((tm, tn), lambda j, i: (i, j)),
            pl.BlockSpec((tm, LANES), z_index),
        ],
        out_shape=[
            jax.ShapeDtypeStruct((T, MAIN_WIDTH), BF16),
            jax.ShapeDtypeStruct((T, LANES), F32),
        ],
        compiler_params=_params(2),
        name="in_proj",
    )(x2, g, w_in_b, w_z)


def _head_meansq(xt):
    r = lax.broadcasted_iota(jnp.int32, (LANES, LANES), 0) // HEAD_DIM
    c = lax.broadcasted_iota(jnp.int32, (LANES, LANES), 1) // HEAD_DIM
    mean_blockdiag = jnp.where(r == c, 1.0 / HEAD_DIM, 0.0).astype(BF16)
    return jnp.dot((xt * xt).astype(BF16), mean_blockdiag, preferred_element_type=F32)


def _head_rmsnorm(xt, g):
    return xt * lax.rsqrt(_head_meansq(xt) + EPS) * g


def _swa_kernel(sink_ref, q_ref, k_ref, v_ref, gq_ref, gk_ref, wu_ref, wd_ref,
                o_ref, wub_ref, wdb_ref,
                qn_ref, kk_ref, vv_ref, tbl_ref, *, rows):
    b = pl.program_id(0)
    n = pl.program_id(1)
    blk = ATTN_BLOCK
    low = lax.broadcasted_iota(jnp.int32, (1, LANES), 1) < HEAD_DIM

    @pl.when(jnp.logical_and(b == 0, n == 0))
    def _():
        qi = lax.broadcasted_iota(jnp.int32, (blk, 2 * blk), 0)
        kj = lax.broadcasted_iota(jnp.int32, (blk, 2 * blk), 1)
        dist = qi + blk - kj
        valid = jnp.logical_and(dist >= 0, dist < blk)
        distf = dist.astype(F32)
        for qh in range(N_Q_HEADS):
            tbl_ref[qh] = jnp.where(valid, (-ALIBI_SLOPES[qh] * LOG2E) * distf, NEG)

    @pl.when(n == 0)
    def _():
        kk_ref[:, 0:blk, :] = jnp.zeros((2 * N_KV_HEADS, blk, LANES), BF16)
        vv_ref[:, 0:blk, :] = jnp.zeros((2 * N_KV_HEADS, blk, LANES), BF16)

    for t in range(ATTN_WIDTH // LANES):
        qt = q_ref[:, t * LANES:(t + 1) * LANES].astype(F32)
        qn = _head_rmsnorm(qt, gq_ref[...])
        qn_ref[:, t * LANES:(t + 1) * LANES] = qn.astype(BF16)

    for p in range(KV_WIDTH // LANES):
        kt = _head_rmsnorm(k_ref[:, p * LANES:(p + 1) * LANES].astype(F32), gk_ref[...])
        vt = v_ref[:, p * LANES:(p + 1) * LANES].astype(F32)
        k_even = jnp.where(low, kt, 0.0)
        k_odd = jnp.where(low, 0.0, kt)
        v_even = jnp.where(low, vt, 1.0)
        v_odd = jnp.where(low, 1.0, vt)
        h0, h1 = 2 * p, 2 * p + 1
        kk_ref[2 * h0 + 0, blk:, :] = k_even.astype(BF16)
        kk_ref[2 * h0 + 1, blk:, :] = pltpu.roll(k_even, HEAD_DIM, 1).astype(BF16)
        kk_ref[2 * h1 + 1, blk:, :] = k_odd.astype(BF16)
        kk_ref[2 * h1 + 0, blk:, :] = pltpu.roll(k_odd, HEAD_DIM, 1).astype(BF16)
        vv_ref[2 * h0 + 0, blk:, :] = v_even.astype(BF16)
        vv_ref[2 * h0 + 1, blk:, :] = pltpu.roll(v_even, HEAD_DIM, 1).astype(BF16)
        vv_ref[2 * h1 + 1, blk:, :] = v_odd.astype(BF16)
        vv_ref[2 * h1 + 0, blk:, :] = pltpu.roll(v_odd, HEAD_DIM, 1).astype(BF16)

    prev_cols = lax.broadcasted_iota(jnp.int32, (1, 2 * blk), 1) < blk

    negrow = jnp.where(jnp.logical_and(n == 0, prev_cols), NEG, 0.0).astype(F32)
    for nb in range(rows // blk):
        r0 = nb * blk
        for t in range(ATTN_WIDTH // LANES):
            h, tt = divmod(t, 2)
            qs = qn_ref[r0:r0 + blk, t * LANES:(t + 1) * LANES]
            res, sink_terms = [], []
            for a in range(2):
                qh = 4 * h + 2 * tt + a
                keys = kk_ref[2 * h + a, r0:r0 + 2 * blk, :]
                s = lax.dot_general(qs, keys, (((1,), (1,)), ((), ())), preferred_element_type=F32)
                s = s + tbl_ref[qh]
                if nb == 0:
                    s = s + negrow
                sink = sink_ref[qh] * LOG2E
                m = jnp.maximum(jnp.max(s, axis=-1, keepdims=True), sink)
                p = jnp.exp2(s - m).astype(BF16)
                vals = vv_ref[2 * h + a, r0:r0 + 2 * blk, :]
                res.append(jnp.dot(p, vals, preferred_element_type=F32))
                sink_terms.append(jnp.exp2(sink - m))
            num = jnp.where(low, res[0], res[1])
            den = pltpu.roll(jnp.where(low, res[1], res[0]), HEAD_DIM, 1)
            den = den + jnp.where(low, sink_terms[0], sink_terms[1])
            o_ref[r0:r0 + blk, t * LANES:(t + 1) * LANES] = (num / den).astype(o_ref.dtype)

    kk_ref[:, 0:blk, :] = kk_ref[:, rows:rows + blk, :]
    vv_ref[:, 0:blk, :] = vv_ref[:, rows:rows + blk, :]

    wub_ref[...] = wu_ref[...].astype(BF16)
    wdb_ref[...] = wd_ref[...].astype(BF16)


def _swa(proj, sinks, gq, gk, w_up, w_down, l, B, S, rows):
    T = B * S
    nsteps = S // rows
    total_steps = B * nsteps
    up_rows = D_MODEL // total_steps
    down_rows = D_FF // total_steps
    assert up_rows % 16 == 0 and up_rows * total_steps == D_MODEL and down_rows * total_steps == D_FF
    qcol = OFF_AQ // ATTN_WIDTH
    kcol = OFF_AK // KV_WIDTH
    vcol = OFF_AV // KV_WIDTH
    return pl.pallas_call(
        functools.partial(_swa_kernel, rows=rows),
        grid=(B, nsteps),
        in_specs=[
            pl.BlockSpec(memory_space=pltpu.SMEM),
            pl.BlockSpec((rows, ATTN_WIDTH), lambda b, n: (b * nsteps + n, qcol)),
            pl.BlockSpec((rows, KV_WIDTH), lambda b, n: (b * nsteps + n, kcol)),
            pl.BlockSpec((rows, KV_WIDTH), lambda b, n: (b * nsteps + n, vcol)),
            pl.BlockSpec((1, LANES), lambda b, n: (0, 0)),
            pl.BlockSpec((1, LANES), lambda b, n: (0, 0)),
            pl.BlockSpec((None, up_rows, D_FF), lambda b, n: (l, b * nsteps + n, 0)),
            pl.BlockSpec((None, down_rows, D_MODEL), lambda b, n: (l, b * nsteps + n, 0)),
        ],
        out_specs=[
            pl.BlockSpec((rows, ATTN_WIDTH), lambda b, n: (b * nsteps + n, 0)),
            pl.BlockSpec((up_rows, D_FF), lambda b, n: (b * nsteps + n, 0)),
            pl.BlockSpec((down_rows, D_MODEL), lambda b, n: (b * nsteps + n, 0)),
        ],
        out_shape=[
            jax.ShapeDtypeStruct((T, ATTN_WIDTH), BF16),
            jax.ShapeDtypeStruct((D_MODEL, D_FF), BF16),
            jax.ShapeDtypeStruct((D_FF, D_MODEL), BF16),
        ],
        scratch_shapes=[
            pltpu.VMEM((rows, ATTN_WIDTH), BF16),
            pltpu.VMEM((2 * N_KV_HEADS, rows + ATTN_BLOCK, LANES), BF16),
            pltpu.VMEM((2 * N_KV_HEADS, rows + ATTN_BLOCK, LANES), BF16),
            pltpu.VMEM((N_Q_HEADS, ATTN_BLOCK, 2 * ATTN_BLOCK), F32),
        ],
        compiler_params=_params(2),
        name="swa",
    )(sinks, proj, proj, proj, gq, gk, w_up, w_down)


def _log_sigmoid(x):
    return jnp.minimum(x, 0.0) - jnp.log(1.0 + jnp.exp(jnp.minimum(x, -x)))


def _silu(x):
    h = 0.5 * x
    return h * (1.0 + jnp.tanh(h))


def _gla_body(q_ref, k_ref, v_ref, r_ref, z_ref, gw_ref, gb_ref, ng_ref, o_ref, s_ref, la_ref, *, rows, c,
              after_scores):
    pair = 2 * GLA_CHUNK

    @pl.when(c == 0)
    def _():
        s_ref[...] = jnp.zeros_like(s_ref)

    ri = lax.broadcasted_iota(jnp.int32, (pair, pair), 0)
    ci = lax.broadcasted_iota(jnp.int32, (pair, pair), 1)
    causal = jnp.logical_and(ri // GLA_CHUNK == ci // GLA_CHUNK, ci <= ri)
    tril2 = jnp.concatenate([causal, causal], axis=1).astype(BF16)
    low = lax.broadcasted_iota(jnp.int32, (1, pair), 1) < GLA_CHUNK
    chunk_a_rows = lax.broadcasted_iota(jnp.int32, (pair, 1), 0) < GLA_CHUNK
    b_rows_a_cols = jnp.logical_and(ri >= GLA_CHUNK, ci < GLA_CHUNK)

    logit = jnp.dot(z_ref[...].astype(BF16), gw_ref[...], preferred_element_type=F32) + gb_ref[...]
    la_ref[...] = _log_sigmoid(logit) * (1.0 / GLA_TAU)

    for pr in range(rows // pair):
        rs = slice(pr * pair, (pr + 1) * pair)
        la = la_ref[rs, :]
        la_hi = la.astype(BF16)
        la_lo = (la - la_hi.astype(F32)).astype(BF16)
        bcum = jnp.dot(tril2, jnp.concatenate([la_hi, la_lo], axis=0), preferred_element_type=F32)
        q = q_ref[rs, :].astype(F32) * (GLA_DK ** -0.5)
        k = k_ref[rs, :].astype(F32)
        q_in32 = q * jnp.exp(bcum)
        q_in = q_in32.astype(BF16)
        k_in = (k * jnp.exp(-bcum)).astype(BF16)
        last_a_row = bcum[GLA_CHUNK - 1:GLA_CHUNK, :]
        q_eff = jnp.where(chunk_a_rows, q_in32, q_in32 * jnp.exp(last_a_row)).astype(BF16)
        k_cross = jnp.where(chunk_a_rows, k * jnp.exp(last_a_row - bcum), 0.0).astype(BF16)

        for h in range(GLA_HEADS):
            kc = slice(h * GLA_DK, (h + 1) * GLA_DK)
            vc = slice(h * GLA_DV, (h + 1) * GLA_DV)
            v = v_ref[rs, vc]
            qh = q_in[:, kc]
            keys = jnp.concatenate([k_in[:, kc], k_cross[:, kc]], axis=0)
            att2 = lax.dot_general(qh, keys, (((1,), (1,)), ((), ())), preferred_element_type=F32)
            after_scores(pr * GLA_HEADS + h)
            att =jnp.where(causal, att2[:, :pair], jnp.where(b_rows_a_cols, att2[:, pair:], 0.0)).astype(BF16)
            state = s_ref[h]
            o = jnp.dot(jnp.concatenate([att, q_eff[:, kc]], axis=1),
                        jnp.concatenate([v, state.astype(BF16)], axis=0), preferred_element_type=F32)

            bt = bcum[:, kc].T
            kt = k[:, kc].T
            last_b = bt[:, pair - 1:pair]
            last_ab = bt[:, GLA_CHUNK - 1:GLA_CHUNK] + last_b
            kst = (kt * jnp.exp(jnp.where(low, last_ab, last_b) - bt)).astype(BF16)
            s_ref[h] = state * jnp.exp(last_ab) + jnp.dot(kst, v, preferred_element_type=F32)

            y = _rmsnorm_rows(o, ng_ref[...])
            gate = _silu(r_ref[rs, vc].astype(F32))
            o_ref[rs, vc] = (y * gate).astype(o_ref.dtype)


class _TwoHalves:
    def __init__(self, lo, hi, half):
        self.lo, self.hi, self.half = lo, hi, half

    def __getitem__(self, idx):
        rows, cols = idx
        if cols.start >= self.half:
            return self.hi[rows, cols.start - self.half:cols.stop - self.half]
        assert cols.stop <= self.half
        return self.lo[rows, cols]


def _gla_out_kernel(q_ref, k_ref, v0_ref, v1_ref, r0_ref, r1_ref, z_ref, gw_ref, gb_ref, ng_ref,
                    x_ref, a_ref, w_ref, xo_ref, s_ref, la_ref, og_ref, og_prev_ref, *, rows, nsteps):
    t = pl.program_id(0)

    @pl.when(t == 0)
    def _():
        og_prev_ref[...] = jnp.zeros_like(og_prev_ref)

    n_units = (rows // (2 * GLA_CHUNK)) * GLA_HEADS
    n_slices = D_MODEL // OUT_SLICE
    assert n_units == 2 * n_slices

    def project_piece(unit):
        cs = slice((unit // 2) * OUT_SLICE, (unit // 2 + 1) * OUT_SLICE)
        if unit % 2 == 0:
            xo_ref[:, cs] = x_ref[:, cs] + jnp.dot(a_ref[...], w_ref[0:ATTN_WIDTH, cs],
                                                   preferred_element_type=F32)
        else:
            xo_ref[:, cs] += jnp.dot(og_prev_ref[...], w_ref[ATTN_WIDTH:, cs], preferred_element_type=F32)

    half = GLA_WIDTH // 2
    _gla_body(q_ref, k_ref, _TwoHalves(v0_ref, v1_ref, half), _TwoHalves(r0_ref, r1_ref, half),
              z_ref, gw_ref, gb_ref, ng_ref, og_ref, s_ref, la_ref, rows=rows, c=lax.rem(t, nsteps),
              after_scores=project_piece)
    og_prev_ref[...] = og_ref[...]


def _gla_out(proj, z, gate_w, gate_b, norm_g, x2, o_attn, w_out, l, B, S, rows):
    T = B * S
    nsteps = S // rows
    ntiles = B * nsteps
    cq = OFF_GQ // GLA_QK_WIDTH
    ck = OFF_GK // GLA_QK_WIDTH
    assert OFF_GV % 512 == 0 and OFF_GR % 512 == 0
    cur = lambda t: jnp.minimum(t, ntiles - 1)
    prev = lambda t: jnp.maximum(t - 1, 0)
    half = GLA_WIDTH // 2
    return pl.pallas_call(
        functools.partial(_gla_out_kernel, rows=rows, nsteps=nsteps),
        grid=(ntiles + 1,),
        in_specs=[
            pl.BlockSpec((rows, GLA_QK_WIDTH), lambda t: (cur(t), cq)),
            pl.BlockSpec((rows, GLA_QK_WIDTH), lambda t: (cur(t), ck)),
            pl.BlockSpec((rows, half), lambda t: (cur(t), OFF_GV // half)),
            pl.BlockSpec((rows, half), lambda t: (cur(t), OFF_GV // half + 1)),
            pl.BlockSpec((rows, half), lambda t: (cur(t), OFF_GR // half)),
            pl.BlockSpec((rows, half), lambda t: (cur(t), OFF_GR // half + 1)),
            pl.BlockSpec((rows, LANES), lambda t: (cur(t), 0)),
            pl.BlockSpec((None, LANES, GLA_QK_WIDTH), lambda t: (l, 0, 0)),
            pl.BlockSpec((None, 1, GLA_QK_WIDTH), lambda t: (l, 0, 0)),
            pl.BlockSpec((None, 1, GLA_DV), lambda t: (l, 0, 0)),
            pl.BlockSpec((rows, D_MODEL), lambda t: (prev(t), 0)),
            pl.BlockSpec((rows, ATTN_WIDTH), lambda t: (prev(t), 0)),
            pl.BlockSpec((None, ATTN_WIDTH + GLA_WIDTH, D_MODEL), lambda t: (l, 0, 0)),
        ],
        out_specs=pl.BlockSpec((rows, D_MODEL), lambda t: (prev(t), 0)),
        out_shape=jax.ShapeDtypeStruct((T, D_MODEL), F32),
        scratch_shapes=[
            pltpu.VMEM((GLA_HEADS, GLA_DK, GLA_DV), F32),
            pltpu.VMEM((rows, GLA_QK_WIDTH), F32),
            pltpu.VMEM((rows, GLA_WIDTH), BF16),
            pltpu.VMEM((rows, GLA_WIDTH), BF16),
        ],
        compiler_params=_params(1),
        name="gla_out",
    )(proj, proj, proj, proj, proj, proj, z, gate_w, gate_b, norm_g, x2, o_attn, w_out)


def _ffn_kernel(x_ref, g_ref, wu_ref, wd_ref, o_ref, hn_ref):
    f = pl.program_id(1)

    @pl.when(f == 0)
    def _():
        x = x_ref[...]
        hn_ref[...] = _rmsnorm_rows(x, g_ref[...]).astype(BF16)
        o_ref[...] = x

    u = jnp.dot(hn_ref[...], wu_ref[...], preferred_element_type=F32)
    u = jnp.maximum(u, 0.0)
    u = (u * u).astype(BF16)
    o_ref[...] += jnp.dot(u, wd_ref[...], preferred_element_type=F32)


def _ffn(x2, g, w_up_b, w_down_b, l, tm, tf):
    T = x2.shape[0]
    return pl.pallas_call(
        _ffn_kernel,
        grid=(T // tm, D_FF // tf),
        in_specs=[
            pl.BlockSpec((tm, D_MODEL), lambda i, f: (i, 0)),
            pl.BlockSpec((None, 1, D_MODEL), lambda i, f: (l, 0, 0)),
            pl.BlockSpec((D_MODEL, tf), lambda i, f: (0, f)),
            pl.BlockSpec((tf, D_MODEL), lambda i, f: (f, 0)),
        ],
        out_specs=pl.BlockSpec((tm, D_MODEL), lambda i, f: (i, 0)),
        out_shape=jax.ShapeDtypeStruct((T, D_MODEL), F32),
        scratch_shapes=[pltpu.VMEM((tm, D_MODEL), BF16)],
        compiler_params=_params(2),
        name="ffn",
    )(x2, g, w_up_b, w_down_b)


def _tile_plan(T, S):
    tm = min(1024, T)
    mix_rows = min(512, S)
    assert T % tm == 0 and S % mix_rows == 0 and mix_rows % (2 * GLA_CHUNK) == 0
    return dict(tm=tm, tn_in=2304, tf=1024, mix_rows=mix_rows)


def kernel(x, norm1_g, w_in, q_norm_g, k_norm_g, attn_sinks, gla_gate_w, gla_gate_b, gla_norm_g,
           w_out, norm2_g, w_up, w_down):
    B, S, D = x.shape
    assert D == D_MODEL
    T = B * S
    plan = _tile_plan(T, S)
    depth = w_in.shape[0]
    x2 = x.reshape(T, D)

    w_in_b = w_in.astype(BF16)
    w_z = jnp.pad(w_in[:, :, MAIN_WIDTH:], ((0, 0), (0, 0), (0, LANES - GLA_RANK))).astype(BF16)
    gate_w = jnp.pad(gla_gate_w, ((0, 0), (0, LANES - GLA_RANK), (0, 0))).astype(BF16)
    norm1 = norm1_g.reshape(depth, 1, D)
    norm2 = norm2_g.reshape(depth, 1, D)
    gate_b = gla_gate_b.reshape(depth, 1, GLA_QK_WIDTH)
    gla_g = gla_norm_g.reshape(depth, 1, GLA_DV)

    for l in range(depth):
        gq = jnp.tile(q_norm_g[l] * (HEAD_DIM ** -0.5 * LOG2E), 2).reshape(1, LANES)
        gk = jnp.tile(k_norm_g[l], 2).reshape(1, LANES)

        proj, z = _in_proj(x2, norm1, w_in_b, w_z, l, plan["tm"], plan["tn_in"])
        o_attn, w_up_b, w_down_b = _swa(proj, attn_sinks[l], gq, gk, w_up, w_down, l, B, S, plan["mix_rows"])
        x2 = _gla_out(proj, z, gate_w, gate_b, gla_g, x2, o_attn, w_out, l, B, S, plan["mix_rows"])
        x2 = _ffn(x2, norm2, w_up_b, w_down_b, l, plan["tm"], plan["tf"])
    return x2.reshape(B, S, D)
```

```python
import functools

import jax
import jax.numpy as jnp
from jax import lax
from jax.experimental import pallas as pl
from jax.experimental.pallas import tpu as pltpu

F32 = jnp.float32
BF16 = jnp.bfloat16

D_MODEL = 2048
HEAD_DIM = 64
N_Q_HEADS = 16
N_KV_HEADS = 4
ATTN_BLOCK = 128
ATTN_WIDTH = N_Q_HEADS * HEAD_DIM
KV_WIDTH = N_KV_HEADS * HEAD_DIM
GLA_HEADS = 4
GLA_DK = 128
GLA_DV = 256
GLA_QK_WIDTH = GLA_HEADS * GLA_DK
GLA_WIDTH = GLA_HEADS * GLA_DV
GLA_RANK = 16
GLA_TAU = 16.0
GLA_CHUNK = 64
D_FF = 4 * D_MODEL
EPS = 1e-6
NEG = -1e30
LOG2E = 1.4426950408889634

LANES = 128
OUT_SLICE = 256
MAIN_WIDTH = 4608
OFF_AQ, OFF_AK, OFF_AV = 0, 1024, 1280
OFF_GQ, OFF_GK, OFF_GV, OFF_GR = 1536, 2048, 2560, 3584

VMEM_LIMIT = 60 * 1024 * 1024

ALIBI_SLOPES = tuple(2.0 ** (-8.0 * (i + 1) / N_Q_HEADS) for i in range(N_Q_HEADS))


def _params(n_axes):
    return pltpu.CompilerParams(dimension_semantics=("arbitrary",) * n_axes, vmem_limit_bytes=VMEM_LIMIT)


def _rmsnorm_rows(x, g):
    ms = jnp.mean(x * x, axis=-1, keepdims=True)
    return (x * lax.rsqrt(ms + EPS)) * g


def _in_proj_kernel(x_ref, g_ref, w_ref, wz_ref, o_ref, z_ref):
    hn = _rmsnorm_rows(x_ref[...], g_ref[...]).astype(BF16)
    o_ref[...] = jnp.dot(hn, w_ref[...], preferred_element_type=F32).astype(o_ref.dtype)

    @pl.when(pl.program_id(0) == 0)
    def _():
        z_ref[...] = jnp.dot(hn, wz_ref[...], preferred_element_type=F32)


def _in_proj(x2, g, w_in_b, w_z, l, tm, tn):
    T = x2.shape[0]
    n_rows = T // tm
    z_index = lambda j, i: (jnp.where(j == 0, i, n_rows - 1), 0)
    return pl.pallas_call(
        _in_proj_kernel,
        grid=(MAIN_WIDTH // tn, n_rows),
        in_specs=[
            pl.BlockSpec((tm, D_MODEL), lambda j, i: (i, 0)),
            pl.BlockSpec((None, 1, D_MODEL), lambda j, i: (l, 0, 0)),
            pl.BlockSpec((None, D_MODEL, tn), lambda j, i: (l, 0, j)),
            pl.BlockSpec((None, D_MODEL, LANES), lambda j, i: (l, 0, 0)),
        ],
        out_specs=[
            pl.BlockSpec((tm, tn), lambda j, i: (i, j)),
            pl.BlockSpec((tm, LANES), z_index),
        ],
        out_shape=[
            jax.ShapeDtypeStruct((T, MAIN_WIDTH), BF16),
            jax.ShapeDtypeStruct((T, LANES), F32),
        ],
        compiler_params=_params(2),
        name="in_proj",
    )(x2, g, w_in_b, w_z)


def _head_meansq(xt):
    r = lax.broadcasted_iota(jnp.int32, (LANES, LANES), 0) // HEAD_DIM
    c = lax.broadcasted_iota(jnp.int32, (LANES, LANES), 1) // HEAD_DIM
    mean_blockdiag = jnp.where(r == c, 1.0 / HEAD_DIM, 0.0).astype(BF16)
    return jnp.dot((xt * xt).astype(BF16), mean_blockdiag, preferred_element_type=F32)


def _head_rmsnorm(xt, g):
    return xt * lax.rsqrt(_head_meansq(xt) + EPS) * g


def _swa_kernel(sink_ref, q_ref, k_ref, v_ref, gq_ref, gk_ref, wu_ref, wd_ref,
                o_ref, wub_ref, wdb_ref,
                qn_ref, kk_ref, vv_ref, tbl_ref, *, rows):
    b = pl.program_id(0)
    n = pl.program_id(1)
    blk = ATTN_BLOCK
    low = lax.broadcasted_iota(jnp.int32, (1, LANES), 1) < HEAD_DIM

    @pl.when(jnp.logical_and(b == 0, n == 0))
    def _():
        qi = lax.broadcasted_iota(jnp.int32, (blk, 2 * blk), 0)
        kj = lax.broadcasted_iota(jnp.int32, (blk, 2 * blk), 1)
        dist = qi + blk - kj
        valid = jnp.logical_and(dist >= 0, dist < blk)
        distf = dist.astype(F32)
        for qh in range(N_Q_HEADS):
            tbl_ref[qh] = jnp.where(valid, (-ALIBI_SLOPES[qh] * LOG2E) * distf, NEG)

    @pl.when(n == 0)
    def _():
        kk_ref[:, 0:blk, :] = jnp.zeros((2 * N_KV_HEADS, blk, LANES), BF16)
        vv_ref[:, 0:blk, :] = jnp.zeros((2 * N_KV_HEADS, blk, LANES), BF16)

    for t in range(ATTN_WIDTH // LANES):
        qt = q_ref[:, t * LANES:(t + 1) * LANES].astype(F32)
        qn = _head_rmsnorm(qt, gq_ref[...])
        qn_ref[:, t * LANES:(t + 1) * LANES] = qn.astype(BF16)

    for p in range(KV_WIDTH // LANES):
        kt = _head_rmsnorm(k_ref[:, p * LANES:(p + 1) * LANES].astype(F32), gk_ref[...])
        vt = v_ref[:, p * LANES:(p + 1) * LANES].astype(F32)
        k_even = jnp.where(low, kt, 0.0)
        k_odd = jnp.where(low, 0.0, kt)
        v_even = jnp.where(low, vt, 1.0)
        v_odd = jnp.where(low, 1.0, vt)
        h0, h1 = 2 * p, 2 * p + 1
        kk_ref[2 * h0 + 0, blk:, :] = k_even.astype(BF16)
        kk_ref[2 * h0 + 1, blk:, :] = pltpu.roll(k_even, HEAD_DIM, 1).astype(BF16)
        kk_ref[2 * h1 + 1, blk:, :] = k_odd.astype(BF16)
        kk_ref[2 * h1 + 0, blk:, :] = pltpu.roll(k_odd, HEAD_DIM, 1).astype(BF16)
        vv_ref[2 * h0 + 0, blk:, :] = v_even.astype(BF16)
        vv_ref[2 * h0 + 1, blk:, :] = pltpu.roll(v_even, HEAD_DIM, 1).astype(BF16)
        vv_ref[2 * h1 + 1, blk:, :] = v_odd.astype(BF16)
        vv_ref[2 * h1 + 0, blk:, :] = pltpu.roll(v_odd, HEAD_DIM, 1).astype(BF16)

    prev_cols = lax.broadcasted_iota(jnp.int32, (1, 2 * blk), 1) < blk

    negrow = jnp.where(jnp.logical_and(n == 0, prev_cols), NEG, 0.0).astype(F32)
    for nb in range(rows // blk):
        r0 = nb * blk
        for t in range(ATTN_WIDTH // LANES):
            h, tt = divmod(t, 2)
            qs = qn_ref[r0:r0 + blk, t * LANES:(t + 1) * LANES]
            res, sink_terms = [], []
            for a in range(2):
                qh = 4 * h + 2 * tt + a
                keys = kk_ref[2 * h + a, r0:r0 + 2 * blk, :]
                s = lax.dot_general(qs, keys, (((1,), (1,)), ((), ())), preferred_element_type=F32)
                s = s + tbl_ref[qh]
                if nb == 0:
                    s = s + negrow
                sink = sink_ref[qh] * LOG2E
                m = jnp.maximum(jnp.max(s, axis=-1, keepdims=True), sink)
                p = jnp.exp2(s - m).astype(BF16)
                vals = vv_ref[2 * h + a, r0:r0 + 2 * blk, :]
                res.append(jnp.dot(p, vals, preferred_element_type=F32))
                sink_terms.append(jnp.exp2(sink - m))
            num = jnp.where(low, res[0], res[1])
            den = pltpu.roll(jnp.where(low, res[1], res[0]), HEAD_DIM, 1)
            den = den + jnp.where(low, sink_terms[0], sink_terms[1])
            o_ref[r0:r0 + blk, t * LANES:(t + 1) * LANES] = (num / den).astype(o_ref.dtype)

    kk_ref[:, 0:blk, :] = kk_ref[:, rows:rows + blk, :]
    vv_ref[:, 0:blk, :] = vv_ref[:, rows:rows + blk, :]

    wub_ref[...] = wu_ref[...].astype(BF16)
    wdb_ref[...] = wd_ref[...].astype(BF16)


def _swa(proj, sinks, gq, gk, w_up, w_down, l, B, S, rows):
    T = B * S
    nsteps = S // rows
    total_steps = B * nsteps
    up_rows = D_MODEL // total_steps
    down_rows = D_FF // total_steps
    assert up_rows % 16 == 0 and up_rows * total_steps == D_MODEL and down_rows * total_steps == D_FF
    qcol = OFF_AQ // ATTN_WIDTH
    kcol = OFF_AK // KV_WIDTH
    vcol = OFF_AV // KV_WIDTH
    return pl.pallas_call(
        functools.partial(_swa_kernel, rows=rows),
        grid=(B, nsteps),
        in_specs=[
            pl.BlockSpec(memory_space=pltpu.SMEM),
            pl.BlockSpec((rows, ATTN_WIDTH), lambda b, n: (b * nsteps + n, qcol)),
            pl.BlockSpec((rows, KV_WIDTH), lambda b, n: (b * nsteps + n, kcol)),
            pl.BlockSpec((rows, KV_WIDTH), lambda b, n: (b * nsteps + n, vcol)),
            pl.BlockSpec((1, LANES), lambda b, n: (0, 0)),
            pl.BlockSpec((1, LANES), lambda b, n: (0, 0)),
            pl.BlockSpec((None, up_rows, D_FF), lambda b, n: (l, b * nsteps + n, 0)),
            pl.BlockSpec((None, down_rows, D_MODEL), lambda b, n: (l, b * nsteps + n, 0)),
        ],
        out_specs=[
            pl.BlockSpec((rows, ATTN_WIDTH), lambda b, n: (b * nsteps + n, 0)),
            pl.BlockSpec((up_rows, D_FF), lambda b, n: (b * nsteps + n, 0)),
            pl.BlockSpec((down_rows, D_MODEL), lambda b, n: (b * nsteps + n, 0)),
        ],
        out_shape=[
            jax.ShapeDtypeStruct((T, ATTN_WIDTH), BF16),
            jax.ShapeDtypeStruct((D_MODEL, D_FF), BF16),
            jax.ShapeDtypeStruct((D_FF, D_MODEL), BF16),
        ],
        scratch_shapes=[
            pltpu.VMEM((rows, ATTN_WIDTH), BF16),
            pltpu.VMEM((2 * N_KV_HEADS, rows + ATTN_BLOCK, LANES), BF16),
            pltpu.VMEM((2 * N_KV_HEADS, rows + ATTN_BLOCK, LANES), BF16),
            pltpu.VMEM((N_Q_HEADS, ATTN_BLOCK, 2 * ATTN_BLOCK), F32),
        ],
        compiler_params=_params(2),
        name="swa",
    )(sinks, proj, proj, proj, gq, gk, w_up, w_down)


def _log_sigmoid(x):
    return jnp.minimum(x, 0.0) - jnp.log(1.0 + jnp.exp(jnp.minimum(x, -x)))


def _silu(x):
    h = 0.5 * x
    return h * (1.0 + jnp.tanh(h))


def _gla_body(q_ref, k_ref, v_refs, r_refs, z_ref, gw_ref, gb_ref, ng_ref, o_ref, s_ref, la_ref, *, rows, c,
              after_scores):
    pair = 2 * GLA_CHUNK

    @pl.when(c == 0)
    def _():
        s_ref[...] = jnp.zeros_like(s_ref)

    ri = lax.broadcasted_iota(jnp.int32, (pair, pair), 0)
    ci = lax.broadcasted_iota(jnp.int32, (pair, pair), 1)
    causal = jnp.logical_and(ri // GLA_CHUNK == ci // GLA_CHUNK, ci <= ri)
    tril2 = jnp.concatenate([causal, causal], axis=1).astype(BF16)
    low = lax.broadcasted_iota(jnp.int32, (1, pair), 1) < GLA_CHUNK
    chunk_a_rows = lax.broadcasted_iota(jnp.int32, (pair, 1), 0) < GLA_CHUNK
    b_rows_a_cols = jnp.logical_and(ri >= GLA_CHUNK, ci < GLA_CHUNK)

    logit = jnp.dot(z_ref[...].astype(BF16), gw_ref[...], preferred_element_type=F32) + gb_ref[...]
    la_ref[...] = _log_sigmoid(logit) * (1.0 / GLA_TAU)

    for pr in range(rows // pair):
        rs = slice(pr * pair, (pr + 1) * pair)
        la = la_ref[rs, :]
        la_hi = la.astype(BF16)
        la_lo = (la - la_hi.astype(F32)).astype(BF16)
        bcum = jnp.dot(tril2, jnp.concatenate([la_hi, la_lo], axis=0), preferred_element_type=F32)
        q = q_ref[rs, :].astype(F32) * (GLA_DK ** -0.5)
        k = k_ref[rs, :].astype(F32)
        q_in32 = q * jnp.exp(bcum)
        q_in = q_in32.astype(BF16)
        k_in = (k * jnp.exp(-bcum)).astype(BF16)
        last_a_row = bcum[GLA_CHUNK - 1:GLA_CHUNK, :]
        q_eff = jnp.where(chunk_a_rows, q_in32, q_in32 * jnp.exp(last_a_row)).astype(BF16)
        k_cross = jnp.where(chunk_a_rows, k * jnp.exp(last_a_row - bcum), 0.0).astype(BF16)

        for h in range(GLA_HEADS):
            kc = slice(h * GLA_DK, (h + 1) * GLA_DK)
            vc = slice(h * GLA_DV, (h + 1) * GLA_DV)
            v = v_refs[h][rs, :]
            qh = q_in[:, kc]
            keys = jnp.concatenate([k_in[:, kc], k_cross[:, kc]], axis=0)
            att2 = lax.dot_general(qh, keys, (((1,), (1,)), ((), ())), preferred_element_type=F32)
            after_scores(pr * GLA_HEADS + h)
            att =jnp.where(causal, att2[:, :pair], jnp.where(b_rows_a_cols, att2[:, pair:], 0.0)).astype(BF16)
            state = s_ref[h]
            o = jnp.dot(jnp.concatenate([att, q_eff[:, kc]], axis=1),
                        jnp.concatenate([v, state.astype(BF16)], axis=0), preferred_element_type=F32)

            bt = bcum[:, kc].T
            kt = k[:, kc].T
            last_b = bt[:, pair - 1:pair]
            last_ab = bt[:, GLA_CHUNK - 1:GLA_CHUNK] + last_b
            kst = (kt * jnp.exp(jnp.where(low, last_ab, last_b) - bt)).astype(BF16)
            s_ref[h] = state * jnp.exp(last_ab) + jnp.dot(kst, v, preferred_element_type=F32)

            y = _rmsnorm_rows(o, ng_ref[...])
            gate = _silu(r_refs[h][rs, :].astype(F32))
            o_ref[rs, vc] = (y * gate).astype(o_ref.dtype)


def _gla_out_kernel(q_ref, k_ref, *refs, rows, nsteps):
    v_refs, r_refs = refs[:GLA_HEADS], refs[GLA_HEADS:2 * GLA_HEADS]
    (z_ref, gw_ref, gb_ref, ng_ref, x_ref, a_ref, w_ref, xo_ref,
     s_ref, la_ref, og_ref, og_prev_ref) = refs[2 * GLA_HEADS:]
    t = pl.program_id(0)

    @pl.when(t == 0)
    def _():
        og_prev_ref[...] = jnp.zeros_like(og_prev_ref)

    n_units = (rows // (2 * GLA_CHUNK)) * GLA_HEADS
    n_slices = D_MODEL // OUT_SLICE
    assert n_units == 2 * n_slices

    def project_piece(unit):
        cs = slice((unit // 2) * OUT_SLICE, (unit // 2 + 1) * OUT_SLICE)
        if unit % 2 == 0:
            xo_ref[:, cs] = x_ref[:, cs] + jnp.dot(a_ref[...], w_ref[0:ATTN_WIDTH, cs],
                                                   preferred_element_type=F32)
        else:
            xo_ref[:, cs] += jnp.dot(og_prev_ref[...], w_ref[ATTN_WIDTH:, cs], preferred_element_type=F32)

    _gla_body(q_ref, k_ref, v_refs, r_refs, z_ref, gw_ref, gb_ref, ng_ref, og_ref, s_ref, la_ref,
              rows=rows, c=lax.rem(t, nsteps), after_scores=project_piece)
    og_prev_ref[...] = og_ref[...]


def _gla_out(proj, z, gate_w, gate_b, norm_g, x2, o_attn, w_out, l, B, S, rows):
    T = B * S
    nsteps = S // rows
    ntiles = B * nsteps
    cq = OFF_GQ // GLA_QK_WIDTH
    ck = OFF_GK // GLA_QK_WIDTH
    assert OFF_GV % GLA_DV == 0 and OFF_GR % GLA_DV == 0
    cur = lambda t: jnp.minimum(t, ntiles - 1)
    prev = lambda t: jnp.maximum(t - 1, 0)
    head_block = lambda off, h: pl.BlockSpec((rows, GLA_DV), lambda t: (cur(t), off // GLA_DV + h))
    return pl.pallas_call(
        functools.partial(_gla_out_kernel, rows=rows, nsteps=nsteps),
        grid=(ntiles + 1,),
        in_specs=[
            pl.BlockSpec((rows, GLA_QK_WIDTH), lambda t: (cur(t), cq)),
            pl.BlockSpec((rows, GLA_QK_WIDTH), lambda t: (cur(t), ck)),
            *[head_block(OFF_GV, h) for h in range(GLA_HEADS)],
            *[head_block(OFF_GR, h) for h in range(GLA_HEADS)],
            pl.BlockSpec((rows, LANES), lambda t: (cur(t), 0)),
            pl.BlockSpec((None, LANES, GLA_QK_WIDTH), lambda t: (l, 0, 0)),
            pl.BlockSpec((None, 1, GLA_QK_WIDTH), lambda t: (l, 0, 0)),
            pl.BlockSpec((None, 1, GLA_DV), lambda t: (l, 0, 0)),
            pl.BlockSpec((rows, D_MODEL), lambda t: (prev(t), 0)),
            pl.BlockSpec((rows, ATTN_WIDTH), lambda t: (prev(t), 0)),
            pl.BlockSpec((None, ATTN_WIDTH + GLA_WIDTH, D_MODEL), lambda t: (l, 0, 0)),
        ],
        out_specs=pl.BlockSpec((rows, D_MODEL), lambda t: (prev(t), 0)),
        out_shape=jax.ShapeDtypeStruct((T, D_MODEL), F32),
        scratch_shapes=[
            pltpu.VMEM((GLA_HEADS, GLA_DK, GLA_DV), F32),
            pltpu.VMEM((rows, GLA_QK_WIDTH), F32),
            pltpu.VMEM((rows, GLA_WIDTH), BF16),
            pltpu.VMEM((rows, GLA_WIDTH), BF16),
        ],
        compiler_params=_params(1),
        name="gla_out",
    )(*([proj] * (2 + 2 * GLA_HEADS)), z, gate_w, gate_b, norm_g, x2, o_attn, w_out)


def _ffn_kernel(x_ref, g_ref, wu_ref, wd_ref, o_ref, hn_ref):
    f = pl.program_id(1)

    @pl.when(f == 0)
    def _():
        x = x_ref[...]
        hn_ref[...] = _rmsnorm_rows(x, g_ref[...]).astype(BF16)
        o_ref[...] = x

    u = jnp.dot(hn_ref[...], wu_ref[...], preferred_element_type=F32)
    u = jnp.maximum(u, 0.0)
    u = (u * u).astype(BF16)
    o_ref[...] += jnp.dot(u, wd_ref[...], preferred_element_type=F32)


def _ffn(x2, g, w_up_b, w_down_b, l, tm, tf):
    T = x2.shape[0]
    return pl.pallas_call(
        _ffn_kernel,
        grid=(T // tm, D_FF // tf),
        in_specs=[
            pl.BlockSpec((tm, D_MODEL), lambda i, f: (i, 0)),
            pl.BlockSpec((None, 1, D_MODEL), lambda i, f: (l, 0, 0)),
            pl.BlockSpec((D_MODEL, tf), lambda i, f: (0, f)),
            pl.BlockSpec((tf, D_MODEL), lambda i, f: (f, 0)),
        ],
        out_specs=pl.BlockSpec((tm, D_MODEL), lambda i, f: (i, 0)),
        out_shape=jax.ShapeDtypeStruct((T, D_MODEL), F32),
        scratch_shapes=[pltpu.VMEM((tm, D_MODEL), BF16)],
        compiler_params=_params(2),
        name="ffn",
    )(x2, g, w_up_b, w_down_b)


def _tile_plan(T, S):
    tm = min(1024, T)
    mix_rows = min(512, S)
    assert T % tm == 0 and S % mix_rows == 0 and mix_rows % (2 * GLA_CHUNK) == 0
    return dict(tm=tm, tn_in=2304, tf=1024, mix_rows=mix_rows)


def kernel(x, norm1_g, w_in, q_norm_g, k_norm_g, attn_sinks, gla_gate_w, gla_gate_b, gla_norm_g,
           w_out, norm2_g, w_up, w_down):
    B, S, D = x.shape
    assert D == D_MODEL
    T = B * S
    plan = _tile_plan(T, S)
    depth = w_in.shape[0]
    x2 = x.reshape(T, D)

    w_in_b = w_in.astype(BF16)
    w_z = jnp.pad(w_in[:, :, MAIN_WIDTH:], ((0, 0), (0, 0), (0, LANES - GLA_RANK))).astype(BF16)
    gate_w = jnp.pad(gla_gate_w, ((0, 0), (0, LANES - GLA_RANK), (0, 0))).astype(BF16)
    norm1 = norm1_g.reshape(depth, 1, D)
    norm2 = norm2_g.reshape(depth, 1, D)
    gate_b = gla_gate_b.reshape(depth, 1, GLA_QK_WIDTH)
    gla_g = gla_norm_g.reshape(depth, 1, GLA_DV)

    for l in range(depth):
        gq = jnp.tile(q_norm_g[l] * (HEAD_DIM ** -0.5 * LOG2E), 2).reshape(1, LANES)
        gk = jnp.tile(k_norm_g[l], 2).reshape(1, LANES)

        proj, z = _in_proj(x2, norm1, w_in_b, w_z, l, plan["tm"], plan["tn_in"])
        o_attn, w_up_b, w_down_b = _swa(proj, attn_sinks[l], gq, gk, w_up, w_down, l, B, S, plan["mix_rows"])
        x2 = _gla_out(proj, z, gate_w, gate_b, gla_g, x2, o_attn, w_out, l, B, S, plan["mix_rows"])
        x2 = _ffn(x2, norm2, w_up_b, w_down_b, l, plan["tm"], plan["tf"])
    return x2.reshape(B, S, D)
```

```python
import functools

import jax
import jax.numpy as jnp
from jax import lax
from jax.experimental import pallas as pl
from jax.experimental.pallas import tpu as pltpu

F32 = jnp.float32
BF16 = jnp.bfloat16

D_MODEL = 2048
HEAD_DIM = 64
N_Q_HEADS = 16
N_KV_HEADS = 4
ATTN_BLOCK = 128
ATTN_WIDTH = N_Q_HEADS * HEAD_DIM
KV_WIDTH = N_KV_HEADS * HEAD_DIM
GLA_HEADS = 4
GLA_DK = 128
GLA_DV = 256
GLA_QK_WIDTH = GLA_HEADS * GLA_DK
GLA_WIDTH = GLA_HEADS * GLA_DV
GLA_RANK = 16
GLA_TAU = 16.0
GLA_CHUNK = 64
D_FF = 4 * D_MODEL
EPS = 1e-6
NEG = -1e30
LOG2E = 1.4426950408889634

LANES = 128
OUT_SLICE = 256
MAIN_WIDTH = 4608
OFF_AQ, OFF_AK, OFF_AV = 0, 1024, 1280
OFF_GQ, OFF_GK, OFF_GV, OFF_GR = 1536, 2048, 2560, 3584

VMEM_LIMIT = 60 * 1024 * 1024

ALIBI_SLOPES = tuple(2.0 ** (-8.0 * (i + 1) / N_Q_HEADS) for i in range(N_Q_HEADS))


def _params(n_axes):
    return pltpu.CompilerParams(dimension_semantics=("arbitrary",) * n_axes, vmem_limit_bytes=VMEM_LIMIT)


def _rmsnorm_rows(x, g):
    ms = jnp.mean(x * x, axis=-1, keepdims=True)
    return (x * lax.rsqrt(ms + EPS)) * g


def _in_proj_kernel(x_ref, g_ref, w_ref, wz_ref, o_ref, z_ref):
    hn = _rmsnorm_rows(x_ref[...], g_ref[...]).astype(BF16)
    o_ref[...] = jnp.dot(hn, w_ref[...], preferred_element_type=F32).astype(o_ref.dtype)

    @pl.when(pl.program_id(0) == 0)
    def _():
        z_ref[...] = jnp.dot(hn, wz_ref[...], preferred_element_type=F32)


def _in_proj(x2, g, w_in_b, w_z, l, tm, tn):
    T = x2.shape[0]
    n_rows = T // tm
    z_index = lambda j, i: (jnp.where(j == 0, i, n_rows - 1), 0)
    return pl.pallas_call(
        _in_proj_kernel,
        grid=(MAIN_WIDTH // tn, n_rows),
        in_specs=[
            pl.BlockSpec((tm, D_MODEL), lambda j, i: (i, 0)),
            pl.BlockSpec((None, 1, D_MODEL), lambda j, i: (l, 0, 0)),
            pl.BlockSpec((None, D_MODEL, tn), lambda j, i: (l, 0, j)),
            pl.BlockSpec((None, D_MODEL, LANES), lambda j, i: (l, 0, 0)),
        ],
        out_specs=[
            pl.BlockSpec((tm, tn), lambda j, i: (i, j)),
            pl.BlockSpec((tm, LANES), z_index),
        ],
        out_shape=[
            jax.ShapeDtypeStruct((T, MAIN_WIDTH), BF16),
            jax.ShapeDtypeStruct((T, LANES), F32),
        ],
        compiler_params=_params(2),
        name="in_proj",
    )(x2, g, w_in_b, w_z)


def _head_meansq(xt):
    r = lax.broadcasted_iota(jnp.int32, (LANES, LANES), 0) // HEAD_DIM
    c = lax.broadcasted_iota(jnp.int32, (LANES, LANES), 1) // HEAD_DIM
    mean_blockdiag = jnp.where(r == c, 1.0 / HEAD_DIM, 0.0).astype(BF16)
    return jnp.dot((xt * xt).astype(BF16), mean_blockdiag, preferred_element_type=F32)


def _head_rmsnorm(xt, g):
    return xt * lax.rsqrt(_head_meansq(xt) + EPS) * g


def _swa_kernel(sink_ref, q_ref, k_ref, v_ref, gq_ref, gk_ref, wu_ref, wd_ref,
                o_ref, wub_ref, wdb_ref,
                qn_ref, kk_ref, vt_ref, tbl_ref, *, rows):
    b = pl.program_id(0)
    n = pl.program_id(1)
    blk = ATTN_BLOCK
    low = lax.broadcasted_iota(jnp.int32, (1, LANES), 1) < HEAD_DIM

    @pl.when(jnp.logical_and(b == 0, n == 0))
    def _():
        kj = lax.broadcasted_iota(jnp.int32, (2 * blk, blk), 0)
        qi = lax.broadcasted_iota(jnp.int32, (2 * blk, blk), 1)
        dist = qi + blk - kj
        valid = jnp.logical_and(dist >= 0, dist < blk)
        distf = dist.astype(F32)
        for qh in range(N_Q_HEADS):
            tbl_ref[qh] = jnp.where(valid, (-ALIBI_SLOPES[qh] * LOG2E) * distf, NEG)

    @pl.when(n == 0)
    def _():
        kk_ref[:, 0:blk, :] = jnp.zeros((2 * N_KV_HEADS, blk, LANES), BF16)
        vt_ref[:, :, 0:blk] = jnp.zeros((N_KV_HEADS, LANES, blk), BF16)

    for t in range(ATTN_WIDTH // LANES):
        qt = q_ref[:, t * LANES:(t + 1) * LANES].astype(F32)
        qn = _head_rmsnorm(qt, gq_ref[...])
        qn_ref[:, t * LANES:(t + 1) * LANES] = qn.astype(BF16)

    for p in range(KV_WIDTH // LANES):
        kt = _head_rmsnorm(k_ref[:, p * LANES:(p + 1) * LANES].astype(F32), gk_ref[...])
        vt = v_ref[:, p * LANES:(p + 1) * LANES].astype(F32)
        k_even = jnp.where(low, kt, 0.0)
        k_odd = jnp.where(low, 0.0, kt)
        h0, h1 = 2 * p, 2 * p + 1
        kk_ref[2 * h0 + 0, blk:, :] = k_even.astype(BF16)
        kk_ref[2 * h0 + 1, blk:, :] = pltpu.roll(k_even, HEAD_DIM, 1).astype(BF16)
        kk_ref[2 * h1 + 1, blk:, :] = k_odd.astype(BF16)
        kk_ref[2 * h1 + 0, blk:, :] = pltpu.roll(k_odd, HEAD_DIM, 1).astype(BF16)
        vt_ref[h0, :, blk:] = jnp.where(low, vt, 1.0).T.astype(BF16)
        vt_ref[h1, :, blk:] = pltpu.roll(jnp.where(low, 1.0, vt), HEAD_DIM, 1).T.astype(BF16)

    prev_keys = lax.broadcasted_iota(jnp.int32, (2 * blk, LANES), 0) < blk
    neg_prev = jnp.where(jnp.logical_and(n == 0, prev_keys), NEG, 0.0).astype(F32)
    for nb in range(rows // blk):
        r0 = nb * blk
        for t in range(ATTN_WIDTH // LANES):
            h, tt = divmod(t, 2)
            qs = qn_ref[r0:r0 + blk, t * LANES:(t + 1) * LANES]
            vals_t = vt_ref[h, :, r0:r0 + 2 * blk]
            normed = []
            for a in range(2):
                qh = 4 * h + 2 * tt + a
                keys = kk_ref[2 * h + a, r0:r0 + 2 * blk, :]
                s = lax.dot_general(keys, qs, (((1,), (1,)), ((), ())), preferred_element_type=F32)
                s = s + tbl_ref[qh]
                if nb == 0:
                    s = s + neg_prev
                sink = sink_ref[qh] * LOG2E
                m = jnp.maximum(jnp.max(s, axis=0, keepdims=True), sink)
                p = jnp.exp2(s - m).astype(BF16)
                r = jnp.dot(vals_t, p, preferred_element_type=F32)
                den = r[HEAD_DIM:HEAD_DIM + 1, :] + jnp.exp2(sink - m)
                normed.append(r[0:HEAD_DIM, :] / den)
            out_t = jnp.concatenate(normed, axis=0)
            o_ref[r0:r0 + blk, t * LANES:(t + 1) * LANES] = out_t.T.astype(o_ref.dtype)

    kk_ref[:, 0:blk, :] = kk_ref[:, rows:rows + blk, :]
    vt_ref[:, :, 0:blk] = vt_ref[:, :, rows:rows + blk]

    wub_ref[...] = wu_ref[...].astype(BF16)
    wdb_ref[...] = wd_ref[...].astype(BF16)


def _swa(proj, sinks, gq, gk, w_up, w_down, l, B, S, rows):
    T = B * S
    nsteps = S // rows
    total_steps = B * nsteps
    up_rows = D_MODEL // total_steps
    down_rows = D_FF // total_steps
    assert up_rows % 16 == 0 and up_rows * total_steps == D_MODEL and down_rows * total_steps == D_FF
    qcol = OFF_AQ // ATTN_WIDTH
    kcol = OFF_AK // KV_WIDTH
    vcol = OFF_AV // KV_WIDTH
    return pl.pallas_call(
        functools.partial(_swa_kernel, rows=rows),
        grid=(B, nsteps),
        in_specs=[
            pl.BlockSpec(memory_space=pltpu.SMEM),
            pl.BlockSpec((rows, ATTN_WIDTH), lambda b, n: (b * nsteps + n, qcol)),
            pl.BlockSpec((rows, KV_WIDTH), lambda b, n: (b * nsteps + n, kcol)),
            pl.BlockSpec((rows, KV_WIDTH), lambda b, n: (b * nsteps + n, vcol)),
            pl.BlockSpec((1, LANES), lambda b, n: (0, 0)),
            pl.BlockSpec((1, LANES), lambda b, n: (0, 0)),
            pl.BlockSpec((None, up_rows, D_FF), lambda b, n: (l, b * nsteps + n, 0)),
            pl.BlockSpec((None, down_rows, D_MODEL), lambda b, n: (l, b * nsteps + n, 0)),
        ],
        out_specs=[
            pl.BlockSpec((rows, ATTN_WIDTH), lambda b, n: (b * nsteps + n, 0)),
            pl.BlockSpec((up_rows, D_FF), lambda b, n: (b * nsteps + n, 0)),
            pl.BlockSpec((down_rows, D_MODEL), lambda b, n: (b * nsteps + n, 0)),
        ],
        out_shape=[
            jax.ShapeDtypeStruct((T, ATTN_WIDTH), BF16),
            jax.ShapeDtypeStruct((D_MODEL, D_FF), BF16),
            jax.ShapeDtypeStruct((D_FF, D_MODEL), BF16),
        ],
        scratch_shapes=[
            pltpu.VMEM((rows, ATTN_WIDTH), BF16),
            pltpu.VMEM((2 * N_KV_HEADS, rows + ATTN_BLOCK, LANES), BF16),
            pltpu.VMEM((N_KV_HEADS, LANES, rows + ATTN_BLOCK), BF16),
            pltpu.VMEM((N_Q_HEADS, 2 * ATTN_BLOCK, ATTN_BLOCK), F32),
        ],
        compiler_params=_params(2),
        name="swa",
    )(sinks, proj, proj, proj, gq, gk, w_up, w_down)


def _log_sigmoid(x):
    return jnp.minimum(x, 0.0) - jnp.log(1.0 + jnp.exp(jnp.minimum(x, -x)))


def _silu(x):
    h = 0.5 * x
    return h * (1.0 + jnp.tanh(h))


def _gla_body(q_ref, k_ref, v_refs, r_refs, z_ref, gw_ref, gb_ref, ng_ref, o_ref, s_ref, la_ref, *, rows, c,
              after_scores):
    pair = 2 * GLA_CHUNK

    @pl.when(c == 0)
    def _():
        s_ref[...] = jnp.zeros_like(s_ref)

    ri = lax.broadcasted_iota(jnp.int32, (pair, pair), 0)
    ci = lax.broadcasted_iota(jnp.int32, (pair, pair), 1)
    causal = jnp.logical_and(ri // GLA_CHUNK == ci // GLA_CHUNK, ci <= ri)
    tril2 = jnp.concatenate([causal, causal], axis=1).astype(BF16)
    low = lax.broadcasted_iota(jnp.int32, (1, pair), 1) < GLA_CHUNK
    chunk_a_rows = lax.broadcasted_iota(jnp.int32, (pair, 1), 0) < GLA_CHUNK
    b_rows_a_cols = jnp.logical_and(ri >= GLA_CHUNK, ci < GLA_CHUNK)

    logit = jnp.dot(z_ref[...].astype(BF16), gw_ref[...], preferred_element_type=F32) + gb_ref[...]
    la_ref[...] = _log_sigmoid(logit) * (1.0 / GLA_TAU)

    for pr in range(rows // pair):
        rs = slice(pr * pair, (pr + 1) * pair)
        la = la_ref[rs, :]
        la_hi = la.astype(BF16)
        la_lo = (la - la_hi.astype(F32)).astype(BF16)
        bcum = jnp.dot(tril2, jnp.concatenate([la_hi, la_lo], axis=0), preferred_element_type=F32)
        q = q_ref[rs, :].astype(F32) * (GLA_DK ** -0.5)
        k = k_ref[rs, :].astype(F32)
        q_in32 = q * jnp.exp(bcum)
        q_in = q_in32.astype(BF16)
        k_in = (k * jnp.exp(-bcum)).astype(BF16)
        last_a_row = bcum[GLA_CHUNK - 1:GLA_CHUNK, :]
        q_eff = jnp.where(chunk_a_rows, q_in32, q_in32 * jnp.exp(last_a_row)).astype(BF16)
        k_cross = jnp.where(chunk_a_rows, k * jnp.exp(last_a_row - bcum), 0.0).astype(BF16)

        for h in range(GLA_HEADS):
            kc = slice(h * GLA_DK, (h + 1) * GLA_DK)
            vc = slice(h * GLA_DV, (h + 1) * GLA_DV)
            v = v_refs[h][rs, :]
            qh = q_in[:, kc]
            keys = jnp.concatenate([k_in[:, kc], k_cross[:, kc]], axis=0)
            att2 = lax.dot_general(qh, keys, (((1,), (1,)), ((), ())), preferred_element_type=F32)
            after_scores(pr * GLA_HEADS + h)
            att =jnp.where(causal, att2[:, :pair], jnp.where(b_rows_a_cols, att2[:, pair:], 0.0)).astype(BF16)
            state = s_ref[h]
            o = jnp.dot(jnp.concatenate([att, q_eff[:, kc]], axis=1),
                        jnp.concatenate([v, state.astype(BF16)], axis=0), preferred_element_type=F32)

            bt = bcum[:, kc].T
            kt = k[:, kc].T
            last_b = bt[:, pair - 1:pair]
            last_ab = bt[:, GLA_CHUNK - 1:GLA_CHUNK] + last_b
            kst = (kt * jnp.exp(jnp.where(low, last_ab, last_b) - bt)).astype(BF16)
            s_ref[h] = state * jnp.exp(last_ab) + jnp.dot(kst, v, preferred_element_type=F32)

            y = _rmsnorm_rows(o, ng_ref[...])
            gate = _silu(r_refs[h][rs, :].astype(F32))
            o_ref[rs, vc] = (y * gate).astype(o_ref.dtype)


def _gla_out_kernel(q_ref, k_ref, *refs, rows, nsteps):
    v_refs, r_refs = refs[:GLA_HEADS], refs[GLA_HEADS:2 * GLA_HEADS]
    (z_ref, gw_ref, gb_ref, ng_ref, x_ref, a_ref, w_ref, xo_ref,
     s_ref, la_ref, og_ref, og_prev_ref) = refs[2 * GLA_HEADS:]
    t = pl.program_id(0)

    @pl.when(t == 0)
    def _():
        og_prev_ref[...] = jnp.zeros_like(og_prev_ref)

    n_units = (rows // (2 * GLA_CHUNK)) * GLA_HEADS
    n_slices = D_MODEL // OUT_SLICE
    assert n_units == 2 * n_slices

    def project_piece(unit):
        cs = slice((unit // 2) * OUT_SLICE, (unit // 2 + 1) * OUT_SLICE)
        if unit % 2 == 0:
            xo_ref[:, cs] = x_ref[:, cs] + jnp.dot(a_ref[...], w_ref[0:ATTN_WIDTH, cs],
                                                   preferred_element_type=F32)
        else:
            xo_ref[:, cs] += jnp.dot(og_prev_ref[...], w_ref[ATTN_WIDTH:, cs], preferred_element_type=F32)

    _gla_body(q_ref, k_ref, v_refs, r_refs, z_ref, gw_ref, gb_ref, ng_ref, og_ref, s_ref, la_ref,
              rows=rows, c=lax.rem(t, nsteps), after_scores=project_piece)
    og_prev_ref[...] = og_ref[...]


def _gla_out(proj, z, gate_w, gate_b, norm_g, x2, o_attn, w_out, l, B, S, rows):
    T = B * S
    nsteps = S // rows
    ntiles = B * nsteps
    cq = OFF_GQ // GLA_QK_WIDTH
    ck = OFF_GK // GLA_QK_WIDTH
    assert OFF_GV % GLA_DV == 0 and OFF_GR % GLA_DV == 0
    cur = lambda t: jnp.minimum(t, ntiles - 1)
    prev = lambda t: jnp.maximum(t - 1, 0)
    head_block = lambda off, h: pl.BlockSpec((rows, GLA_DV), lambda t: (cur(t), off // GLA_DV + h))
    return pl.pallas_call(
        functools.partial(_gla_out_kernel, rows=rows, nsteps=nsteps),
        grid=(ntiles + 1,),
        in_specs=[
            pl.BlockSpec((rows, GLA_QK_WIDTH), lambda t: (cur(t), cq)),
            pl.BlockSpec((rows, GLA_QK_WIDTH), lambda t: (cur(t), ck)),
            *[head_block(OFF_GV, h) for h in range(GLA_HEADS)],
            *[head_block(OFF_GR, h) for h in range(GLA_HEADS)],
            pl.BlockSpec((rows, LANES), lambda t: (cur(t), 0)),
            pl.BlockSpec((None, LANES, GLA_QK_WIDTH), lambda t: (l, 0, 0)),
            pl.BlockSpec((None, 1, GLA_QK_WIDTH), lambda t: (l, 0, 0)),
            pl.BlockSpec((None, 1, GLA_DV), lambda t: (l, 0, 0)),
            pl.BlockSpec((rows, D_MODEL), lambda t: (prev(t), 0)),
            pl.BlockSpec((rows, ATTN_WIDTH), lambda t: (prev(t), 0)),
            pl.BlockSpec((None, ATTN_WIDTH + GLA_WIDTH, D_MODEL), lambda t: (l, 0, 0)),
        ],
        out_specs=pl.BlockSpec((rows, D_MODEL), lambda t: (prev(t), 0)),
        out_shape=jax.ShapeDtypeStruct((T, D_MODEL), F32),
        scratch_shapes=[
            pltpu.VMEM((GLA_HEADS, GLA_DK, GLA_DV), F32),
            pltpu.VMEM((rows, GLA_QK_WIDTH), F32),
            pltpu.VMEM((rows, GLA_WIDTH), BF16),
            pltpu.VMEM((rows, GLA_WIDTH), BF16),
        ],
        compiler_params=_params(1),
        name="gla_out",
    )(*([proj] * (2 + 2 * GLA_HEADS)), z, gate_w, gate_b, norm_g, x2, o_attn, w_out)


def _ffn_kernel(x_ref, g_ref, wu_ref, wd_ref, o_ref, hn_ref):
    f = pl.program_id(1)

    @pl.when(f == 0)
    def _():
        x = x_ref[...]
        hn_ref[...] = _rmsnorm_rows(x, g_ref[...]).astype(BF16)
        o_ref[...] = x

    u = jnp.dot(hn_ref[...], wu_ref[...], preferred_element_type=F32)
    u = jnp.maximum(u, 0.0)
    u = (u * u).astype(BF16)
    o_ref[...] += jnp.dot(u, wd_ref[...], preferred_element_type=F32)


def _ffn(x2, g, w_up_b, w_down_b, l, tm, tf):
    T = x2.shape[0]
    return pl.pallas_call(
        _ffn_kernel,
        grid=(T // tm, D_FF // tf),
        in_specs=[
            pl.BlockSpec((tm, D_MODEL), lambda i, f: (i, 0)),
            pl.BlockSpec((None, 1, D_MODEL), lambda i, f: (l, 0, 0)),
            pl.BlockSpec((D_MODEL, tf), lambda i, f: (0, f)),
            pl.BlockSpec((tf, D_MODEL), lambda i, f: (f, 0)),
        ],
        out_specs=pl.BlockSpec((tm, D_MODEL), lambda i, f: (i, 0)),
        out_shape=jax.ShapeDtypeStruct((T, D_MODEL), F32),
        scratch_shapes=[pltpu.VMEM((tm, D_MODEL), BF16)],
        compiler_params=_params(2),
        name="ffn",
    )(x2, g, w_up_b, w_down_b)


def _tile_plan(T, S):
    tm = min(1024, T)
    mix_rows = min(512, S)
    assert T % tm == 0 and S % mix_rows == 0 and mix_rows % (2 * GLA_CHUNK) == 0
    return dict(tm=tm, tn_in=2304, tf=1024, mix_rows=mix_rows)


def kernel(x, norm1_g, w_in, q_norm_g, k_norm_g, attn_sinks, gla_gate_w, gla_gate_b, gla_norm_g,
           w_out, norm2_g, w_up, w_down):
    B, S, D = x.shape
    assert D == D_MODEL
    T = B * S
    plan = _tile_plan(T, S)
    depth = w_in.shape[0]
    x2 = x.reshape(T, D)

    w_in_b = w_in.astype(BF16)
    w_z = jnp.pad(w_in[:, :, MAIN_WIDTH:], ((0, 0), (0, 0), (0, LANES - GLA_RANK))).astype(BF16)
    gate_w = jnp.pad(gla_gate_w, ((0, 0), (0, LANES - GLA_RANK), (0, 0))).astype(BF16)
    norm1 = norm1_g.reshape(depth, 1, D)
    norm2 = norm2_g.reshape(depth, 1, D)
    gate_b = gla_gate_b.reshape(depth, 1, GLA_QK_WIDTH)
    gla_g = gla_norm_g.reshape(depth, 1, GLA_DV)

    for l in range(depth):
        gq = jnp.tile(q_norm_g[l] * (HEAD_DIM ** -0.5 * LOG2E), 2).reshape(1, LANES)
        gk = jnp.tile(k_norm_g[l], 2).reshape(1, LANES)

        proj, z = _in_proj(x2, norm1, w_in_b, w_z, l, plan["tm"], plan["tn_in"])
        o_attn, w_up_b, w_down_b = _swa(proj, attn_sinks[l], gq, gk, w_up, w_down, l, B, S, plan["mix_rows"])
        x2 = _gla_out(proj, z, gate_w, gate_b, gla_g, x2, o_attn, w_out, l, B, S, plan["mix_rows"])
        x2 = _ffn(x2, norm2, w_up_b, w_down_b, l, plan["tm"], plan["tf"])
    return x2.reshape(B, S, D)
```

```python
import functools

import jax
import jax.numpy as jnp
from jax import lax
from jax.experimental import pallas as pl
from jax.experimental.pallas import tpu as pltpu

F32 = jnp.float32
BF16 = jnp.bfloat16

D_MODEL = 2048
HEAD_DIM = 64
N_Q_HEADS = 16
N_KV_HEADS = 4
ATTN_BLOCK = 128
ATTN_WIDTH = N_Q_HEADS * HEAD_DIM
KV_WIDTH = N_KV_HEADS * HEAD_DIM
GLA_HEADS = 4
GLA_DK = 128
GLA_DV = 256
GLA_QK_WIDTH = GLA_HEADS * GLA_DK
GLA_WIDTH = GLA_HEADS * GLA_DV
GLA_RANK = 16
GLA_TAU = 16.0
GLA_CHUNK = 64
D_FF = 4 * D_MODEL
EPS = 1e-6
NEG = -1e30
LOG2E = 1.4426950408889634

LANES = 128
OUT_SLICE = 256
MAIN_WIDTH = 4608
OFF_AQ, OFF_AK, OFF_AV = 0, 1024, 1280
OFF_GQ, OFF_GK, OFF_GV, OFF_GR = 1536, 2048, 2560, 3584

VMEM_LIMIT = 60 * 1024 * 1024

ALIBI_SLOPES = tuple(2.0 ** (-8.0 * (i + 1) / N_Q_HEADS) for i in range(N_Q_HEADS))


def _params(n_axes):
    return pltpu.CompilerParams(dimension_semantics=("arbitrary",) * n_axes, vmem_limit_bytes=VMEM_LIMIT)


def _rmsnorm_rows(x, g):
    ms = jnp.mean(x * x, axis=-1, keepdims=True)
    return (x * lax.rsqrt(ms + EPS)) * g


def _in_proj_kernel(x_ref, g_ref, w_ref, wz_ref, o_ref, z_ref):
    hn = _rmsnorm_rows(x_ref[...], g_ref[...]).astype(BF16)
    o_ref[...] = jnp.dot(hn, w_ref[...], preferred_element_type=F32).astype(o_ref.dtype)

    @pl.when(pl.program_id(0) == 0)
    def _():
        z_ref[...] = jnp.dot(hn, wz_ref[...], preferred_element_type=F32)


def _in_proj(x2, g, w_in_b, w_z, l, tm, tn):
    T = x2.shape[0]
    n_rows = T // tm
    z_index = lambda j, i: (jnp.where(j == 0, i, n_rows - 1), 0)
    return pl.pallas_call(
        _in_proj_kernel,
        grid=(MAIN_WIDTH // tn, n_rows),
        in_specs=[
            pl.BlockSpec((tm, D_MODEL), lambda j, i: (i, 0)),
            pl.BlockSpec((None, 1, D_MODEL), lambda j, i: (l, 0, 0)),
            pl.BlockSpec((None, D_MODEL, tn), lambda j, i: (l, 0, j)),
            pl.BlockSpec((None, D_MODEL, LANES), lambda j, i: (l, 0, 0)),
        ],
        out_specs=[
            pl.BlockSpec((tm, tn), lambda j, i: (i, j)),
            pl.BlockSpec((tm, LANES), z_index),
        ],
        out_shape=[
            jax.ShapeDtypeStruct((T, MAIN_WIDTH), BF16),
            jax.ShapeDtypeStruct((T, LANES), F32),
        ],
        compiler_params=_params(2),
        name="in_proj",
    )(x2, g, w_in_b, w_z)


def _head_meansq(xt):
    r = lax.broadcasted_iota(jnp.int32, (LANES, LANES), 0) // HEAD_DIM
    c = lax.broadcasted_iota(jnp.int32, (LANES, LANES), 1) // HEAD_DIM
    mean_blockdiag = jnp.where(r == c, 1.0 / HEAD_DIM, 0.0).astype(BF16)
    return jnp.dot((xt * xt).astype(BF16), mean_blockdiag, preferred_element_type=F32)


def _head_rmsnorm(xt, g):
    return xt * lax.rsqrt(_head_meansq(xt) + EPS) * g


def _swa_kernel(sink_ref, q_ref, k_ref, v_ref, gq_ref, gk_ref, wu_ref, wd_ref,
                o_ref, wub_ref, wdb_ref,
                qn_ref, kk_ref, vt_ref, tbl_ref, *, rows):
    b = pl.program_id(0)
    n = pl.program_id(1)
    blk = ATTN_BLOCK
    low = lax.broadcasted_iota(jnp.int32, (1, LANES), 1) < HEAD_DIM

    @pl.when(jnp.logical_and(b == 0, n == 0))
    def _():
        kj = lax.broadcasted_iota(jnp.int32, (2 * blk, blk), 0)
        qi = lax.broadcasted_iota(jnp.int32, (2 * blk, blk), 1)
        dist = qi + blk - kj
        valid = jnp.logical_and(dist >= 0, dist < blk)
        distf = dist.astype(F32)
        for qh in range(N_Q_HEADS):
            tbl_ref[qh] = jnp.where(valid, (-ALIBI_SLOPES[qh] * LOG2E) * distf, NEG)

    @pl.when(n == 0)
    def _():
        kk_ref[:, 0:blk, :] = jnp.zeros((2 * N_KV_HEADS, blk, LANES), BF16)
        vt_ref[:, :, 0:blk] = jnp.zeros((N_KV_HEADS, LANES, blk), BF16)

    for t in range(ATTN_WIDTH // LANES):
        qt = q_ref[:, t * LANES:(t + 1) * LANES].astype(F32)
        qn = _head_rmsnorm(qt, gq_ref[...])
        qn_ref[:, t * LANES:(t + 1) * LANES] = qn.astype(BF16)

    for p in range(KV_WIDTH // LANES):
        kt = _head_rmsnorm(k_ref[:, p * LANES:(p + 1) * LANES].astype(F32), gk_ref[...])
        vt = v_ref[:, p * LANES:(p + 1) * LANES].astype(F32)
        k_even = jnp.where(low, kt, 0.0)
        k_odd = jnp.where(low, 0.0, kt)
        h0, h1 = 2 * p, 2 * p + 1
        kk_ref[2 * h0 + 0, blk:, :] = k_even.astype(BF16)
        kk_ref[2 * h0 + 1, blk:, :] = pltpu.roll(k_even, HEAD_DIM, 1).astype(BF16)
        kk_ref[2 * h1 + 1, blk:, :] = k_odd.astype(BF16)
        kk_ref[2 * h1 + 0, blk:, :] = pltpu.roll(k_odd, HEAD_DIM, 1).astype(BF16)
        vt_ref[h0, :, blk:] = jnp.where(low, vt, 1.0).T.astype(BF16)
        vt_ref[h1, :, blk:] = pltpu.roll(jnp.where(low, 1.0, vt), HEAD_DIM, 1).T.astype(BF16)

    prev_keys = lax.broadcasted_iota(jnp.int32, (2 * blk, LANES), 0) < blk
    neg_prev = jnp.where(jnp.logical_and(n == 0, prev_keys), NEG, 0.0).astype(F32)
    for nb in range(rows // blk):
        r0 = nb * blk
        for t in range(ATTN_WIDTH // LANES):
            h, tt = divmod(t, 2)
            qs = qn_ref[r0:r0 + blk, t * LANES:(t + 1) * LANES]
            vals_t = vt_ref[h, :, r0:r0 + 2 * blk]
            normed = []
            for a in range(2):
                qh = 4 * h + 2 * tt + a
                keys = kk_ref[2 * h + a, r0:r0 + 2 * blk, :]
                s = lax.dot_general(keys, qs, (((1,), (1,)), ((), ())), preferred_element_type=F32)
                s = s + tbl_ref[qh]
                if nb == 0:
                    s = s + neg_prev
                sink = sink_ref[qh] * LOG2E
                m = jnp.maximum(jnp.max(s, axis=0, keepdims=True), sink)
                p = jnp.exp2(s - m).astype(BF16)
                r = jnp.dot(vals_t, p, preferred_element_type=F32)
                den = r[HEAD_DIM:HEAD_DIM + 1, :] + jnp.exp2(sink - m)
                normed.append(r[0:HEAD_DIM, :] / den)
            out_t = jnp.concatenate(normed, axis=0)
            o_ref[r0:r0 + blk, t * LANES:(t + 1) * LANES] = out_t.T.astype(o_ref.dtype)

    kk_ref[:, 0:blk, :] = kk_ref[:, rows:rows + blk, :]
    vt_ref[:, :, 0:blk] = vt_ref[:, :, rows:rows + blk]

    wub_ref[...] = wu_ref[...].astype(BF16)
    wdb_ref[...] = wd_ref[...].astype(BF16)


def _swa(proj, sinks, gq, gk, w_up, w_down, l, B, S, rows):
    T = B * S
    nsteps = S // rows
    total_steps = B * nsteps
    up_rows = D_MODEL // total_steps
    down_rows = D_FF // total_steps
    assert up_rows % 16 == 0 and up_rows * total_steps == D_MODEL and down_rows * total_steps == D_FF
    qcol = OFF_AQ // ATTN_WIDTH
    kcol = OFF_AK // KV_WIDTH
    vcol = OFF_AV // KV_WIDTH
    return pl.pallas_call(
        functools.partial(_swa_kernel, rows=rows),
        grid=(B, nsteps),
        in_specs=[
            pl.BlockSpec(memory_space=pltpu.SMEM),
            pl.BlockSpec((rows, ATTN_WIDTH), lambda b, n: (b * nsteps + n, qcol)),
            pl.BlockSpec((rows, KV_WIDTH), lambda b, n: (b * nsteps + n, kcol)),
            pl.BlockSpec((rows, KV_WIDTH), lambda b, n: (b * nsteps + n, vcol)),
            pl.BlockSpec((1, LANES), lambda b, n: (0, 0)),
            pl.BlockSpec((1, LANES), lambda b, n: (0, 0)),
            pl.BlockSpec((None, up_rows, D_FF), lambda b, n: (l, b * nsteps + n, 0)),
            pl.BlockSpec((None, down_rows, D_MODEL), lambda b, n: (l, b * nsteps + n, 0)),
        ],
        out_specs=[
            pl.BlockSpec((rows, ATTN_WIDTH), lambda b, n: (b * nsteps + n, 0)),
            pl.BlockSpec((up_rows, D_FF), lambda b, n: (b * nsteps + n, 0)),
            pl.BlockSpec((down_rows, D_MODEL), lambda b, n: (b * nsteps + n, 0)),
        ],
        out_shape=[
            jax.ShapeDtypeStruct((T, ATTN_WIDTH), BF16),
            jax.ShapeDtypeStruct((D_MODEL, D_FF), BF16),
            jax.ShapeDtypeStruct((D_FF, D_MODEL), BF16),
        ],
        scratch_shapes=[
            pltpu.VMEM((rows, ATTN_WIDTH), BF16),
            pltpu.VMEM((2 * N_KV_HEADS, rows + ATTN_BLOCK, LANES), BF16),
            pltpu.VMEM((N_KV_HEADS, LANES, rows + ATTN_BLOCK), BF16),
            pltpu.VMEM((N_Q_HEADS, 2 * ATTN_BLOCK, ATTN_BLOCK), F32),
        ],
        compiler_params=_params(2),
        name="swa",
    )(sinks, proj, proj, proj, gq, gk, w_up, w_down)


def _log_sigmoid(x):
    return jnp.minimum(x, 0.0) - jnp.log(1.0 + jnp.exp(jnp.minimum(x, -x)))


def _silu(x):
    h = 0.5 * x
    return h * (1.0 + jnp.tanh(h))


def _gla_body(q_ref, k_ref, v_refs, r_refs, z_ref, gw_ref, gb_ref, ng_ref, o_ref, s_ref, la_ref, *, rows, c,
              after_scores):
    pair = 2 * GLA_CHUNK

    @pl.when(c == 0)
    def _():
        s_ref[...] = jnp.zeros_like(s_ref)

    ri = lax.broadcasted_iota(jnp.int32, (pair, pair), 0)
    ci = lax.broadcasted_iota(jnp.int32, (pair, pair), 1)
    causal = jnp.logical_and(ri // GLA_CHUNK == ci // GLA_CHUNK, ci <= ri)
    tril2 = jnp.concatenate([causal, causal], axis=1).astype(BF16)
    low = lax.broadcasted_iota(jnp.int32, (1, pair), 1) < GLA_CHUNK
    chunk_a_rows = lax.broadcasted_iota(jnp.int32, (pair, 1), 0) < GLA_CHUNK
    b_rows_a_cols = jnp.logical_and(ri >= GLA_CHUNK, ci < GLA_CHUNK)

    logit = jnp.dot(z_ref[...].astype(BF16), gw_ref[...], preferred_element_type=F32) + gb_ref[...]
    la_ref[...] = _log_sigmoid(logit) * (1.0 / GLA_TAU)

    for pr in range(rows // pair):
        rs = slice(pr * pair, (pr + 1) * pair)
        la = la_ref[rs, :]
        la_hi = la.astype(BF16)
        la_lo = (la - la_hi.astype(F32)).astype(BF16)
        bcum = jnp.dot(tril2, jnp.concatenate([la_hi, la_lo], axis=0), preferred_element_type=F32)
        q = q_ref[rs, :].astype(F32) * (GLA_DK ** -0.5)
        k = k_ref[rs, :].astype(F32)
        q_in32 = q * jnp.exp(bcum)
        q_in = q_in32.astype(BF16)
        k_in = (k * jnp.exp(-bcum)).astype(BF16)
        last_a_row = bcum[GLA_CHUNK - 1:GLA_CHUNK, :]
        q_eff = jnp.where(chunk_a_rows, q_in32, q_in32 * jnp.exp(last_a_row)).astype(BF16)
        k_cross = jnp.where(chunk_a_rows, k * jnp.exp(last_a_row - bcum), 0.0).astype(BF16)

        for h in range(GLA_HEADS):
            kc = slice(h * GLA_DK, (h + 1) * GLA_DK)
            vc = slice(h * GLA_DV, (h + 1) * GLA_DV)
            v = v_refs[h][rs, :]
            qh = q_in[:, kc]
            keys = jnp.concatenate([k_in[:, kc], k_cross[:, kc]], axis=0)
            att2 = lax.dot_general(qh, keys, (((1,), (1,)), ((), ())), preferred_element_type=F32)
            after_scores(pr * GLA_HEADS + h)
            att =jnp.where(causal, att2[:, :pair], jnp.where(b_rows_a_cols, att2[:, pair:], 0.0)).astype(BF16)
            state = s_ref[h]
            o = jnp.dot(jnp.concatenate([att, q_eff[:, kc]], axis=1),
                        jnp.concatenate([v, state.astype(BF16)], axis=0), preferred_element_type=F32)

            bt = bcum[:, kc].T
            kt = k[:, kc].T
            last_b = bt[:, pair - 1:pair]
            last_ab = bt[:, GLA_CHUNK - 1:GLA_CHUNK] + last_b
            kst = (kt * jnp.exp(jnp.where(low, last_ab, last_b) - bt)).astype(BF16)
            s_ref[h] = state * jnp.exp(last_ab) + jnp.dot(kst, v, preferred_element_type=F32)

            y = _rmsnorm_rows(o, ng_ref[...])
            gate = _silu(r_refs[h][rs, :].astype(F32))
            o_ref[rs, vc] = (y * gate).astype(o_ref.dtype)


def _gla_out_kernel(q_ref, k_ref, *refs, rows, nsteps):
    v_refs, r_refs = refs[:GLA_HEADS], refs[GLA_HEADS:2 * GLA_HEADS]
    (z_ref, gw_ref, gb_ref, ng_ref, x_ref, a_ref, w_ref, xo_ref,
     s_ref, la_ref, og_ref, og_prev_ref) = refs[2 * GLA_HEADS:]
    t = pl.program_id(0)

    @pl.when(t == 0)
    def _():
        og_prev_ref[...] = jnp.zeros_like(og_prev_ref)

    n_units = (rows // (2 * GLA_CHUNK)) * GLA_HEADS
    n_slices = D_MODEL // OUT_SLICE
    assert n_units == 2 * n_slices

    def project_piece(unit):
        cs = slice((unit // 2) * OUT_SLICE, (unit // 2 + 1) * OUT_SLICE)
        if unit % 2 == 0:
            xo_ref[:, cs] = x_ref[:, cs] + jnp.dot(a_ref[...], w_ref[0:ATTN_WIDTH, cs],
                                                   preferred_element_type=F32)
        else:
            xo_ref[:, cs] += jnp.dot(og_prev_ref[...], w_ref[ATTN_WIDTH:, cs], preferred_element_type=F32)

    _gla_body(q_ref, k_ref, v_refs, r_refs, z_ref, gw_ref, gb_ref, ng_ref, og_ref, s_ref, la_ref,
              rows=rows, c=lax.rem(t, nsteps), after_scores=project_piece)
    og_prev_ref[...] = og_ref[...]


def _gla_out(proj, z, gate_w, gate_b, norm_g, x2, o_attn, w_out, l, B, S, rows):
    T = B * S
    nsteps = S // rows
    ntiles = B * nsteps
    cq = OFF_GQ // GLA_QK_WIDTH
    ck = OFF_GK // GLA_QK_WIDTH
    assert OFF_GV % GLA_DV == 0 and OFF_GR % GLA_DV == 0
    cur = lambda t: jnp.minimum(t, ntiles - 1)
    prev = lambda t: jnp.maximum(t - 1, 0)
    head_block = lambda off, h: pl.BlockSpec((rows, GLA_DV), lambda t: (cur(t), off // GLA_DV + h))
    return pl.pallas_call(
        functools.partial(_gla_out_kernel, rows=rows, nsteps=nsteps),
        grid=(ntiles + 1,),
        in_specs=[
            pl.BlockSpec((rows, GLA_QK_WIDTH), lambda t: (cur(t), cq)),
            pl.BlockSpec((rows, GLA_QK_WIDTH), lambda t: (cur(t), ck)),
            *[head_block(OFF_GV, h) for h in range(GLA_HEADS)],
            *[head_block(OFF_GR, h) for h in range(GLA_HEADS)],
            pl.BlockSpec((rows, LANES), lambda t: (cur(t), 0)),
            pl.BlockSpec((None, LANES, GLA_QK_WIDTH), lambda t: (l, 0, 0)),
            pl.BlockSpec((None, 1, GLA_QK_WIDTH), lambda t: (l, 0, 0)),
            pl.BlockSpec((None, 1, GLA_DV), lambda t: (l, 0, 0)),
            pl.BlockSpec((rows, D_MODEL), lambda t: (prev(t), 0)),
            pl.BlockSpec((rows, ATTN_WIDTH), lambda t: (prev(t), 0)),
            pl.BlockSpec((None, ATTN_WIDTH + GLA_WIDTH, D_MODEL), lambda t: (l, 0, 0)),
        ],
        out_specs=pl.BlockSpec((rows, D_MODEL), lambda t: (prev(t), 0)),
        out_shape=jax.ShapeDtypeStruct((T, D_MODEL), F32),
        scratch_shapes=[
            pltpu.VMEM((GLA_HEADS, GLA_DK, GLA_DV), F32),
            pltpu.VMEM((rows, GLA_QK_WIDTH), F32),
            pltpu.VMEM((rows, GLA_WIDTH), BF16),
            pltpu.VMEM((rows, GLA_WIDTH), BF16),
        ],
        compiler_params=_params(1),
        name="gla_out",
    )(*([proj] * (2 + 2 * GLA_HEADS)), z, gate_w, gate_b, norm_g, x2, o_attn, w_out)


def _ffn_kernel(x_ref, g_ref, wu_ref, wd_ref, o_ref, hn_ref):
    f = pl.program_id(1)

    @pl.when(f == 0)
    def _():
        x = x_ref[...]
        hn_ref[...] = _rmsnorm_rows(x, g_ref[...]).astype(BF16)
        o_ref[...] = x

    u = jnp.dot(hn_ref[...], wu_ref[...], preferred_element_type=F32)
    u = jnp.maximum(u, 0.0)
    u = (u * u).astype(BF16)
    o_ref[...] += jnp.dot(u, wd_ref[...], preferred_element_type=F32)


def _ffn(x2, g, w_up_b, w_down_b, l, tm, tf):
    T = x2.shape[0]
    return pl.pallas_call(
        _ffn_kernel,
        grid=(T // tm, D_FF // tf),
        in_specs=[
            pl.BlockSpec((tm, D_MODEL), lambda i, f: (i, 0)),
            pl.BlockSpec((None, 1, D_MODEL), lambda i, f: (l, 0, 0)),
            pl.BlockSpec((D_MODEL, tf), lambda i, f: (0, f)),
            pl.BlockSpec((tf, D_MODEL), lambda i, f: (f, 0)),
        ],
        out_specs=pl.BlockSpec((tm, D_MODEL), lambda i, f: (i, 0)),
        out_shape=jax.ShapeDtypeStruct((T, D_MODEL), F32),
        scratch_shapes=[pltpu.VMEM((tm, D_MODEL), BF16)],
        compiler_params=_params(2),
        name="ffn",
    )(x2, g, w_up_b, w_down_b)


def _tile_plan(T, S):
    tm = min(1024, T)
    swa_rows = min(1024, S)
    mix_rows = min(512, S)
    assert T % tm == 0 and S % swa_rows == 0 and S % mix_rows == 0 and mix_rows % (2 * GLA_CHUNK) == 0
    return dict(tm=tm, tn_in=2304, tf=1024, swa_rows=swa_rows, mix_rows=mix_rows)


def kernel(x, norm1_g, w_in, q_norm_g, k_norm_g, attn_sinks, gla_gate_w, gla_gate_b, gla_norm_g,
           w_out, norm2_g, w_up, w_down):
    B, S, D = x.shape
    assert D == D_MODEL
    T = B * S
    plan = _tile_plan(T, S)
    depth = w_in.shape[0]
    x2 = x.reshape(T, D)

    w_in_b = w_in.astype(BF16)
    w_z = jnp.pad(w_in[:, :, MAIN_WIDTH:], ((0, 0), (0, 0), (0, LANES - GLA_RANK))).astype(BF16)
    gate_w = jnp.pad(gla_gate_w, ((0, 0), (0, LANES - GLA_RANK), (0, 0))).astype(BF16)
    norm1 = norm1_g.reshape(depth, 1, D)
    norm2 = norm2_g.reshape(depth, 1, D)
    gate_b = gla_gate_b.reshape(depth, 1, GLA_QK_WIDTH)
    gla_g = gla_norm_g.reshape(depth, 1, GLA_DV)

    for l in range(depth):
        gq = jnp.tile(q_norm_g[l] * (HEAD_DIM ** -0.5 * LOG2E), 2).reshape(1, LANES)
        gk = jnp.tile(k_norm_g[l], 2).reshape(1, LANES)

        proj, z = _in_proj(x2, norm1, w_in_b, w_z, l, plan["tm"], plan["tn_in"])
        o_attn, w_up_b, w_down_b = _swa(proj, attn_sinks[l], gq, gk, w_up, w_down, l, B, S, plan["swa_rows"])
        x2 = _gla_out(proj, z, gate_w, gate_b, gla_g, x2, o_attn, w_out, l, B, S, plan["mix_rows"])
        x2 = _ffn(x2, norm2, w_up_b, w_down_b, l, plan["tm"], plan["tf"])
    return x2.reshape(B, S, D)
```

```python
import functools

import jax
import jax.numpy as jnp
from jax import lax
from jax.experimental import pallas as pl
from jax.experimental.pallas import tpu as pltpu

F32 = jnp.float32
BF16 = jnp.bfloat16

D_MODEL = 2048
HEAD_DIM = 64
N_Q_HEADS = 16
N_KV_HEADS = 4
ATTN_BLOCK = 128
ATTN_WIDTH = N_Q_HEADS * HEAD_DIM
KV_WIDTH = N_KV_HEADS * HEAD_DIM
GLA_HEADS = 4
GLA_DK = 128
GLA_DV = 256
GLA_QK_WIDTH = GLA_HEADS * GLA_DK
GLA_WIDTH = GLA_HEADS * GLA_DV
GLA_RANK = 16
GLA_TAU = 16.0
GLA_CHUNK = 64
D_FF = 4 * D_MODEL
EPS = 1e-6
NEG = -1e30
LOG2E = 1.4426950408889634

LANES = 128
OUT_SLICE = 256
MAIN_WIDTH = 4608
OFF_AQ, OFF_AK, OFF_AV = 0, 1024, 1280
OFF_GQ, OFF_GK, OFF_GV, OFF_GR = 1536, 2048, 2560, 3584

VMEM_LIMIT = 60 * 1024 * 1024

ALIBI_SLOPES = tuple(2.0 ** (-8.0 * (i + 1) / N_Q_HEADS) for i in range(N_Q_HEADS))


def _params(n_axes):
    return pltpu.CompilerParams(dimension_semantics=("arbitrary",) * n_axes, vmem_limit_bytes=VMEM_LIMIT)


def _rmsnorm_rows(x, g):
    ms = jnp.mean(x * x, axis=-1, keepdims=True)
    return (x * lax.rsqrt(ms + EPS)) * g


def _in_proj_kernel(x_ref, g_ref, w_ref, wz_ref, o_ref, z_ref):
    hn = _rmsnorm_rows(x_ref[...], g_ref[...]).astype(BF16)
    o_ref[...] = jnp.dot(hn, w_ref[...], preferred_element_type=F32).astype(o_ref.dtype)

    @pl.when(pl.program_id(0) == 0)
    def _():
        z_ref[...] = jnp.dot(hn, wz_ref[...], preferred_element_type=F32)


def _in_proj(x2, g, w_in_b, w_z, l, tm, tn):
    T = x2.shape[0]
    n_rows = T // tm
    z_index = lambda j, i: (jnp.where(j == 0, i, n_rows - 1), 0)
    return pl.pallas_call(
        _in_proj_kernel,
        grid=(MAIN_WIDTH // tn, n_rows),
        in_specs=[
            pl.BlockSpec((tm, D_MODEL), lambda j, i: (i, 0)),
            pl.BlockSpec((None, 1, D_MODEL), lambda j, i: (l, 0, 0)),
            pl.BlockSpec((None, D_MODEL, tn), lambda j, i: (l, 0, j)),
            pl.BlockSpec((None, D_MODEL, LANES), lambda j, i: (l, 0, 0)),
        ],
        out_specs=[
            pl.BlockSpec((tm, tn), lambda j, i: (i, j)),
            pl.BlockSpec((tm, LANES), z_index),
        ],
        out_shape=[
            jax.ShapeDtypeStruct((T, MAIN_WIDTH), BF16),
            jax.ShapeDtypeStruct((T, LANES), F32),
        ],
        compiler_params=_params(2),
        name="in_proj",
    )(x2, g, w_in_b, w_z)


def _head_meansq(xt):
    r = lax.broadcasted_iota(jnp.int32, (LANES, LANES), 0) // HEAD_DIM
    c = lax.broadcasted_iota(jnp.int32, (LANES, LANES), 1) // HEAD_DIM
    mean_blockdiag = jnp.where(r == c, 1.0 / HEAD_DIM, 0.0).astype(BF16)
    return jnp.dot((xt * xt).astype(BF16), mean_blockdiag, preferred_element_type=F32)


def _head_rmsnorm(xt, g):
    return xt * lax.rsqrt(_head_meansq(xt) + EPS) * g


def _swa_kernel(sink_ref, q_ref, k_ref, v_ref, gq_ref, gk_ref, wu_ref, wd_ref,
                o_ref, wub_ref, wdb_ref,
                qn_ref, kk_ref, vt_ref, tbl_ref, *, rows):
    b = pl.program_id(0)
    n = pl.program_id(1)
    blk = ATTN_BLOCK
    low = lax.broadcasted_iota(jnp.int32, (1, LANES), 1) < HEAD_DIM

    @pl.when(jnp.logical_and(b == 0, n == 0))
    def _():
        kj = lax.broadcasted_iota(jnp.int32, (2 * blk, blk), 0)
        qi = lax.broadcasted_iota(jnp.int32, (2 * blk, blk), 1)
        dist = qi + blk - kj
        valid = jnp.logical_and(dist >= 0, dist < blk)
        distf = dist.astype(F32)
        for qh in range(N_Q_HEADS):
            tbl_ref[qh] = jnp.where(valid, (-ALIBI_SLOPES[qh] * LOG2E) * distf, NEG)

    @pl.when(n == 0)
    def _():
        kk_ref[:, 0:blk, :] = jnp.zeros((2 * N_KV_HEADS, blk, LANES), BF16)
        vt_ref[:, :, 0:blk] = jnp.zeros((N_KV_HEADS, LANES, blk), BF16)

    for t in range(ATTN_WIDTH // LANES):
        qt = q_ref[:, t * LANES:(t + 1) * LANES].astype(F32)
        qn = _head_rmsnorm(qt, gq_ref[...])
        qn_ref[:, t * LANES:(t + 1) * LANES] = qn.astype(BF16)

    for p in range(KV_WIDTH // LANES):
        kt = _head_rmsnorm(k_ref[:, p * LANES:(p + 1) * LANES].astype(F32), gk_ref[...])
        vt = v_ref[:, p * LANES:(p + 1) * LANES].astype(F32)
        k_even = jnp.where(low, kt, 0.0)
        k_odd = jnp.where(low, 0.0, kt)
        h0, h1 = 2 * p, 2 * p + 1
        kk_ref[2 * h0 + 0, blk:, :] = k_even.astype(BF16)
        kk_ref[2 * h0 + 1, blk:, :] = pltpu.roll(k_even, HEAD_DIM, 1).astype(BF16)
        kk_ref[2 * h1 + 1, blk:, :] = k_odd.astype(BF16)
        kk_ref[2 * h1 + 0, blk:, :] = pltpu.roll(k_odd, HEAD_DIM, 1).astype(BF16)
        vt_ref[h0, :, blk:] = jnp.where(low, vt, 1.0).T.astype(BF16)
        vt_ref[h1, :, blk:] = pltpu.roll(jnp.where(low, 1.0, vt), HEAD_DIM, 1).T.astype(BF16)

    prev_keys = lax.broadcasted_iota(jnp.int32, (2 * blk, LANES), 0) < blk
    neg_prev = jnp.where(jnp.logical_and(n == 0, prev_keys), NEG, 0.0).astype(F32)
    for nb in range(rows // blk):
        r0 = nb * blk
        for t in range(ATTN_WIDTH // LANES):
            h, tt = divmod(t, 2)
            qs = qn_ref[r0:r0 + blk, t * LANES:(t + 1) * LANES]
            vals_t = vt_ref[h, :, r0:r0 + 2 * blk]
            normed = []
            for a in range(2):
                qh = 4 * h + 2 * tt + a
                keys = kk_ref[2 * h + a, r0:r0 + 2 * blk, :]
                s = lax.dot_general(keys, qs, (((1,), (1,)), ((), ())), preferred_element_type=F32)
                s = s + tbl_ref[qh]
                if nb == 0:
                    s = s + neg_prev
                sink = sink_ref[qh] * LOG2E
                m = jnp.maximum(jnp.max(s, axis=0, keepdims=True), sink)
                p = jnp.exp2(s - m).astype(BF16)
                r = jnp.dot(vals_t, p, preferred_element_type=F32)
                den = r[HEAD_DIM:HEAD_DIM + 1, :] + jnp.exp2(sink - m)
                normed.append(r[0:HEAD_DIM, :] / den)
            out_t = jnp.concatenate(normed, axis=0)
            o_ref[r0:r0 + blk, t * LANES:(t + 1) * LANES] = out_t.T.astype(o_ref.dtype)

    kk_ref[:, 0:blk, :] = kk_ref[:, rows:rows + blk, :]
    vt_ref[:, :, 0:blk] = vt_ref[:, :, rows:rows + blk]

    wub_ref[...] = wu_ref[...].astype(BF16)
    wdb_ref[...] = wd_ref[...].astype(BF16)


def _swa(proj, sinks, gq, gk, w_up, w_down, l, B, S, rows):
    T = B * S
    nsteps = S // rows
    total_steps = B * nsteps
    up_rows = D_MODEL // total_steps
    down_rows = D_FF // total_steps
    assert up_rows % 16 == 0 and up_rows * total_steps == D_MODEL and down_rows * total_steps == D_FF
    qcol = OFF_AQ // ATTN_WIDTH
    kcol = OFF_AK // KV_WIDTH
    vcol = OFF_AV // KV_WIDTH
    return pl.pallas_call(
        functools.partial(_swa_kernel, rows=rows),
        grid=(B, nsteps),
        in_specs=[
            pl.BlockSpec(memory_space=pltpu.SMEM),
            pl.BlockSpec((rows, ATTN_WIDTH), lambda b, n: (b * nsteps + n, qcol)),
            pl.BlockSpec((rows, KV_WIDTH), lambda b, n: (b * nsteps + n, kcol)),
            pl.BlockSpec((rows, KV_WIDTH), lambda b, n: (b * nsteps + n, vcol)),
            pl.BlockSpec((1, LANES), lambda b, n: (0, 0)),
            pl.BlockSpec((1, LANES), lambda b, n: (0, 0)),
            pl.BlockSpec((None, up_rows, D_FF), lambda b, n: (l, b * nsteps + n, 0)),
            pl.BlockSpec((None, down_rows, D_MODEL), lambda b, n: (l, b * nsteps + n, 0)),
        ],
        out_specs=[
            pl.BlockSpec((rows, ATTN_WIDTH), lambda b, n: (b * nsteps + n, 0)),
            pl.BlockSpec((up_rows, D_FF), lambda b, n: (b * nsteps + n, 0)),
            pl.BlockSpec((down_rows, D_MODEL), lambda b, n: (b * nsteps + n, 0)),
        ],
        out_shape=[
            jax.ShapeDtypeStruct((T, ATTN_WIDTH), BF16),
            jax.ShapeDtypeStruct((D_MODEL, D_FF), BF16),
            jax.ShapeDtypeStruct((D_FF, D_MODEL), BF16),
        ],
        scratch_shapes=[
            pltpu.VMEM((rows, ATTN_WIDTH), BF16),
            pltpu.VMEM((2 * N_KV_HEADS, rows + ATTN_BLOCK, LANES), BF16),
            pltpu.VMEM((N_KV_HEADS, LANES, rows + ATTN_BLOCK), BF16),
            pltpu.VMEM((N_Q_HEADS, 2 * ATTN_BLOCK, ATTN_BLOCK), F32),
        ],
        compiler_params=_params(2),
        name="swa",
    )(sinks, proj, proj, proj, gq, gk, w_up, w_down)


def _log_sigmoid(x):
    return jnp.minimum(x, 0.0) - jnp.log(1.0 + jnp.exp(jnp.minimum(x, -x)))


def _silu(x):
    h = 0.5 * x
    return h * (1.0 + jnp.tanh(h))


def _gla_body(q_ref, k_ref, v_refs, r_refs, z_ref, gw_ref, gb_ref, ng_ref, o_ref, s_ref, la_ref, *, rows, c,
              mxu_filler):
    pair = 2 * GLA_CHUNK

    @pl.when(c == 0)
    def _():
        s_ref[...] = jnp.zeros_like(s_ref)

    ri = lax.broadcasted_iota(jnp.int32, (pair, pair), 0)
    ci = lax.broadcasted_iota(jnp.int32, (pair, pair), 1)
    causal = jnp.logical_and(ri // GLA_CHUNK == ci // GLA_CHUNK, ci <= ri)
    tril2 = jnp.concatenate([causal, causal], axis=1).astype(BF16)
    low = lax.broadcasted_iota(jnp.int32, (1, pair), 1) < GLA_CHUNK
    chunk_a_rows = lax.broadcasted_iota(jnp.int32, (pair, 1), 0) < GLA_CHUNK
    b_rows_a_cols = jnp.logical_and(ri >= GLA_CHUNK, ci < GLA_CHUNK)

    logit = jnp.dot(z_ref[...].astype(BF16), gw_ref[...], preferred_element_type=F32) + gb_ref[...]
    la_ref[...] = _log_sigmoid(logit) * (1.0 / GLA_TAU)

    for pr in range(rows // pair):
        rs = slice(pr * pair, (pr + 1) * pair)
        la = la_ref[rs, :]
        la_hi = la.astype(BF16)
        la_lo = (la - la_hi.astype(F32)).astype(BF16)
        bcum = jnp.dot(tril2, jnp.concatenate([la_hi, la_lo], axis=0), preferred_element_type=F32)
        mxu_filler(1)
        q = q_ref[rs, :].astype(F32) * (GLA_DK ** -0.5)
        k = k_ref[rs, :].astype(F32)
        q_in32 = q * jnp.exp(bcum)
        q_in = q_in32.astype(BF16)
        k_in = (k * jnp.exp(-bcum)).astype(BF16)
        last_a_row = bcum[GLA_CHUNK - 1:GLA_CHUNK, :]
        q_eff = jnp.where(chunk_a_rows, q_in32, q_in32 * jnp.exp(last_a_row)).astype(BF16)
        k_cross = jnp.where(chunk_a_rows, k * jnp.exp(last_a_row - bcum), 0.0).astype(BF16)

        for h in range(GLA_HEADS):
            kc = slice(h * GLA_DK, (h + 1) * GLA_DK)
            vc = slice(h * GLA_DV, (h + 1) * GLA_DV)
            v = v_refs[h][rs, :]
            qh = q_in[:, kc]
            keys = jnp.concatenate([k_in[:, kc], k_cross[:, kc]], axis=0)
            att2 = lax.dot_general(qh, keys, (((1,), (1,)), ((), ())), preferred_element_type=F32)
            if h + 1 < GLA_HEADS:
                mxu_filler(1)
            att =jnp.where(causal, att2[:, :pair], jnp.where(b_rows_a_cols, att2[:, pair:], 0.0)).astype(BF16)
            state = s_ref[h]
            o = jnp.dot(jnp.concatenate([att, q_eff[:, kc]], axis=1),
                        jnp.concatenate([v, state.astype(BF16)], axis=0), preferred_element_type=F32)

            bt = bcum[:, kc].T
            kt = k[:, kc].T
            last_b = bt[:, pair - 1:pair]
            last_ab = bt[:, GLA_CHUNK - 1:GLA_CHUNK] + last_b
            kst = (kt * jnp.exp(jnp.where(low, last_ab, last_b) - bt)).astype(BF16)
            s_ref[h] = state * jnp.exp(last_ab) + jnp.dot(kst, v, preferred_element_type=F32)

            y = _rmsnorm_rows(o, ng_ref[...])
            gate = _silu(r_refs[h][rs, :].astype(F32))
            o_ref[rs, vc] = (y * gate).astype(o_ref.dtype)


def _gla_out_kernel(q_ref, k_ref, *refs, rows, nsteps):
    v_refs, r_refs = refs[:GLA_HEADS], refs[GLA_HEADS:2 * GLA_HEADS]
    (z_ref, gw_ref, gb_ref, ng_ref, x_ref, a_ref, w_ref, xo_ref,
     s_ref, la_ref, og_ref, og_prev_ref) = refs[2 * GLA_HEADS:]
    t = pl.program_id(0)

    @pl.when(t == 0)
    def _():
        og_prev_ref[...] = jnp.zeros_like(og_prev_ref)

    pieces = iter(range(2 * (D_MODEL // OUT_SLICE)))

    def mxu_filler(count):
        for piece in (next(pieces, None) for _ in range(count)):
            if piece is None:
                return
            cs = slice((piece // 2) * OUT_SLICE, (piece // 2 + 1) * OUT_SLICE)
            if piece % 2 == 0:
                xo_ref[:, cs] = x_ref[:, cs] + jnp.dot(a_ref[...], w_ref[0:ATTN_WIDTH, cs],
                                                       preferred_element_type=F32)
            else:
                xo_ref[:, cs] += jnp.dot(og_prev_ref[...], w_ref[ATTN_WIDTH:, cs],
                                         preferred_element_type=F32)

    _gla_body(q_ref, k_ref, v_refs, r_refs, z_ref, gw_ref, gb_ref, ng_ref, og_ref, s_ref, la_ref,
              rows=rows, c=lax.rem(t, nsteps), mxu_filler=mxu_filler)
    assert next(pieces, None) is None, "the GLA body must offer enough filler slots for the whole projection"
    og_prev_ref[...] = og_ref[...]


def _gla_out(proj, z, gate_w, gate_b, norm_g, x2, o_attn, w_out, l, B, S, rows):
    T = B * S
    nsteps = S // rows
    ntiles = B * nsteps
    cq = OFF_GQ // GLA_QK_WIDTH
    ck = OFF_GK // GLA_QK_WIDTH
    assert OFF_GV % GLA_DV == 0 and OFF_GR % GLA_DV == 0
    cur = lambda t: jnp.minimum(t, ntiles - 1)
    prev = lambda t: jnp.maximum(t - 1, 0)
    head_block = lambda off, h: pl.BlockSpec((rows, GLA_DV), lambda t: (cur(t), off // GLA_DV + h))
    return pl.pallas_call(
        functools.partial(_gla_out_kernel, rows=rows, nsteps=nsteps),
        grid=(ntiles + 1,),
        in_specs=[
            pl.BlockSpec((rows, GLA_QK_WIDTH), lambda t: (cur(t), cq)),
            pl.BlockSpec((rows, GLA_QK_WIDTH), lambda t: (cur(t), ck)),
            *[head_block(OFF_GV, h) for h in range(GLA_HEADS)],
            *[head_block(OFF_GR, h) for h in range(GLA_HEADS)],
            pl.BlockSpec((rows, LANES), lambda t: (cur(t), 0)),
            pl.BlockSpec((None, LANES, GLA_QK_WIDTH), lambda t: (l, 0, 0)),
            pl.BlockSpec((None, 1, GLA_QK_WIDTH), lambda t: (l, 0, 0)),
            pl.BlockSpec((None, 1, GLA_DV), lambda t: (l, 0, 0)),
            pl.BlockSpec((rows, D_MODEL), lambda t: (prev(t), 0)),
            pl.BlockSpec((rows, ATTN_WIDTH), lambda t: (prev(t), 0)),
            pl.BlockSpec((None, ATTN_WIDTH + GLA_WIDTH, D_MODEL), lambda t: (l, 0, 0)),
        ],
        out_specs=pl.BlockSpec((rows, D_MODEL), lambda t: (prev(t), 0)),
        out_shape=jax.ShapeDtypeStruct((T, D_MODEL), F32),
        scratch_shapes=[
            pltpu.VMEM((GLA_HEADS, GLA_DK, GLA_DV), F32),
            pltpu.VMEM((rows, GLA_QK_WIDTH), F32),
            pltpu.VMEM((rows, GLA_WIDTH), BF16),
            pltpu.VMEM((rows, GLA_WIDTH), BF16),
        ],
        compiler_params=_params(1),
        name="gla_out",
    )(*([proj] * (2 + 2 * GLA_HEADS)), z, gate_w, gate_b, norm_g, x2, o_attn, w_out)


def _ffn_kernel(x_ref, g_ref, wu_ref, wd_ref, o_ref, hn_ref):
    f = pl.program_id(1)

    @pl.when(f == 0)
    def _():
        x = x_ref[...]
        hn_ref[...] = _rmsnorm_rows(x, g_ref[...]).astype(BF16)
        o_ref[...] = x

    u = jnp.dot(hn_ref[...], wu_ref[...], preferred_element_type=F32)
    u = jnp.maximum(u, 0.0)
    u = (u * u).astype(BF16)
    o_ref[...] += jnp.dot(u, wd_ref[...], preferred_element_type=F32)


def _ffn(x2, g, w_up_b, w_down_b, l, tm, tf):
    T = x2.shape[0]
    return pl.pallas_call(
        _ffn_kernel,
        grid=(T // tm, D_FF // tf),
        in_specs=[
            pl.BlockSpec((tm, D_MODEL), lambda i, f: (i, 0)),
            pl.BlockSpec((None, 1, D_MODEL), lambda i, f: (l, 0, 0)),
            pl.BlockSpec((D_MODEL, tf), lambda i, f: (0, f)),
            pl.BlockSpec((tf, D_MODEL), lambda i, f: (f, 0)),
        ],
        out_specs=pl.BlockSpec((tm, D_MODEL), lambda i, f: (i, 0)),
        out_shape=jax.ShapeDtypeStruct((T, D_MODEL), F32),
        scratch_shapes=[pltpu.VMEM((tm, D_MODEL), BF16)],
        compiler_params=_params(2),
        name="ffn",
    )(x2, g, w_up_b, w_down_b)


def _tile_plan(T, S):
    tm = min(1024, T)
    swa_rows = min(1024, S)
    mix_rows = min(512, S)
    assert T % tm == 0 and S % swa_rows == 0 and S % mix_rows == 0 and mix_rows % (2 * GLA_CHUNK) == 0
    return dict(tm=tm, tn_in=2304, tf=1024, swa_rows=swa_rows, mix_rows=mix_rows)


def kernel(x, norm1_g, w_in, q_norm_g, k_norm_g, attn_sinks, gla_gate_w, gla_gate_b, gla_norm_g,
           w_out, norm2_g, w_up, w_down):
    B, S, D = x.shape
    assert D == D_MODEL
    T = B * S
    plan = _tile_plan(T, S)
    depth = w_in.shape[0]
    x2 = x.reshape(T, D)

    w_in_b = w_in.astype(BF16)
    w_z = jnp.pad(w_in[:, :, MAIN_WIDTH:], ((0, 0), (0, 0), (0, LANES - GLA_RANK))).astype(BF16)
    gate_w = jnp.pad(gla_gate_w, ((0, 0), (0, LANES - GLA_RANK), (0, 0))).astype(BF16)
    norm1 = norm1_g.reshape(depth, 1, D)
    norm2 = norm2_g.reshape(depth, 1, D)
    gate_b = gla_gate_b.reshape(depth, 1, GLA_QK_WIDTH)
    gla_g = gla_norm_g.reshape(depth, 1, GLA_DV)

    for l in range(depth):
        gq = jnp.tile(q_norm_g[l] * (HEAD_DIM ** -0.5 * LOG2E), 2).reshape(1, LANES)
        gk = jnp.tile(k_norm_g[l], 2).reshape(1, LANES)

        proj, z = _in_proj(x2, norm1, w_in_b, w_z, l, plan["tm"], plan["tn_in"])
        o_attn, w_up_b, w_down_b = _swa(proj, attn_sinks[l], gq, gk, w_up, w_down, l, B, S, plan["swa_rows"])
        x2 = _gla_out(proj, z, gate_w, gate_b, gla_g, x2, o_attn, w_out, l, B, S, plan["mix_rows"])
        x2 = _ffn(x2, norm2, w_up_b, w_down_b, l, plan["tm"], plan["tf"])
    return x2.reshape(B, S, D)
```

```python
import functools

import jax
import jax.numpy as jnp
from jax import lax
from jax.experimental import pallas as pl
from jax.experimental.pallas import tpu as pltpu

F32 = jnp.float32
BF16 = jnp.bfloat16

D_MODEL = 2048
HEAD_DIM = 64
N_Q_HEADS = 16
N_KV_HEADS = 4
ATTN_BLOCK = 128
ATTN_WIDTH = N_Q_HEADS * HEAD_DIM
KV_WIDTH = N_KV_HEADS * HEAD_DIM
GLA_HEADS = 4
GLA_DK = 128
GLA_DV = 256
GLA_QK_WIDTH = GLA_HEADS * GLA_DK
GLA_WIDTH = GLA_HEADS * GLA_DV
GLA_RANK = 16
GLA_TAU = 16.0
GLA_CHUNK = 64
D_FF = 4 * D_MODEL
EPS = 1e-6
NEG = -1e30
LOG2E = 1.4426950408889634

LANES = 128
OUT_SLICE = 256
MAIN_WIDTH = 4608
OFF_AQ, OFF_AK, OFF_AV = 0, 1024, 1280
OFF_GQ, OFF_GK, OFF_GV, OFF_GR = 1536, 2048, 2560, 3584

VMEM_LIMIT = 60 * 1024 * 1024

ALIBI_SLOPES = tuple(2.0 ** (-8.0 * (i + 1) / N_Q_HEADS) for i in range(N_Q_HEADS))


def _params(n_axes):
    return pltpu.CompilerParams(dimension_semantics=("arbitrary",) * n_axes, vmem_limit_bytes=VMEM_LIMIT)


def _rmsnorm_rows(x, g):
    ms = jnp.mean(x * x, axis=-1, keepdims=True)
    return (x * lax.rsqrt(ms + EPS)) * g


def _in_proj_kernel(x_ref, g_ref, w_ref, wz_ref, o_ref, z_ref):
    hn = _rmsnorm_rows(x_ref[...], g_ref[...]).astype(BF16)
    o_ref[...] = jnp.dot(hn, w_ref[...], preferred_element_type=F32).astype(o_ref.dtype)

    @pl.when(pl.program_id(0) == 0)
    def _():
        z_ref[...] = jnp.dot(hn, wz_ref[...], preferred_element_type=F32)


def _in_proj(x2, g, w_in_b, w_z, l, tm, tn):
    T = x2.shape[0]
    n_rows = T // tm
    z_index = lambda j, i: (jnp.where(j == 0, i, n_rows - 1), 0)
    return pl.pallas_call(
        _in_proj_kernel,
        grid=(MAIN_WIDTH // tn, n_rows),
        in_specs=[
            pl.BlockSpec((tm, D_MODEL), lambda j, i: (i, 0)),
            pl.BlockSpec((None, 1, D_MODEL), lambda j, i: (l, 0, 0)),
            pl.BlockSpec((None, D_MODEL, tn), lambda j, i: (l, 0, j)),
            pl.BlockSpec((None, D_MODEL, LANES), lambda j, i: (l, 0, 0)),
        ],
        out_specs=[
            pl.BlockSpec((tm, tn), lambda j, i: (i, j)),
            pl.BlockSpec((tm, LANES), z_index),
        ],
        out_shape=[
            jax.ShapeDtypeStruct((T, MAIN_WIDTH), BF16),
            jax.ShapeDtypeStruct((T, LANES), F32),
        ],
        compiler_params=_params(2),
        name="in_proj",
    )(x2, g, w_in_b, w_z)


def _head_meansq(xt):
    r = lax.broadcasted_iota(jnp.int32, (LANES, LANES), 0) // HEAD_DIM
    c = lax.broadcasted_iota(jnp.int32, (LANES, LANES), 1) // HEAD_DIM
    mean_blockdiag = jnp.where(r == c, 1.0 / HEAD_DIM, 0.0).astype(BF16)
    return jnp.dot((xt * xt).astype(BF16), mean_blockdiag, preferred_element_type=F32)


def _head_rmsnorm(xt, g):
    return xt * lax.rsqrt(_head_meansq(xt) + EPS) * g


def _swa_kernel(sink_ref, q_ref, k_ref, v_ref, gq_ref, gk_ref, wu_ref, wd_ref,
                o_ref, wub_ref, wdb_ref,
                qn_ref, kk_ref, vt_ref, tbl_ref, *, rows):
    b = pl.program_id(0)
    n = pl.program_id(1)
    blk = ATTN_BLOCK
    low = lax.broadcasted_iota(jnp.int32, (1, LANES), 1) < HEAD_DIM

    @pl.when(jnp.logical_and(b == 0, n == 0))
    def _():
        kj = lax.broadcasted_iota(jnp.int32, (2 * blk, blk), 0)
        qi = lax.broadcasted_iota(jnp.int32, (2 * blk, blk), 1)
        dist = qi + blk - kj
        valid = jnp.logical_and(dist >= 0, dist < blk)
        distf = dist.astype(F32)
        for qh in range(N_Q_HEADS):
            tbl_ref[qh] = jnp.where(valid, (-ALIBI_SLOPES[qh] * LOG2E) * distf, NEG)

    @pl.when(n == 0)
    def _():
        kk_ref[:, 0:blk, :] = jnp.zeros((2 * N_KV_HEADS, blk, LANES), BF16)
        vt_ref[:, :, 0:blk] = jnp.zeros((N_KV_HEADS, LANES, blk), BF16)

    for t in range(ATTN_WIDTH // LANES):
        qt = q_ref[:, t * LANES:(t + 1) * LANES].astype(F32)
        qn = _head_rmsnorm(qt, gq_ref[...])
        qn_ref[:, t * LANES:(t + 1) * LANES] = qn.astype(BF16)

    for p in range(KV_WIDTH // LANES):
        kt = _head_rmsnorm(k_ref[:, p * LANES:(p + 1) * LANES].astype(F32), gk_ref[...])
        vt = v_ref[:, p * LANES:(p + 1) * LANES].astype(F32)
        k_even = jnp.where(low, kt, 0.0)
        k_odd = jnp.where(low, 0.0, kt)
        h0, h1 = 2 * p, 2 * p + 1
        kk_ref[2 * h0 + 0, blk:, :] = k_even.astype(BF16)
        kk_ref[2 * h0 + 1, blk:, :] = pltpu.roll(k_even, HEAD_DIM, 1).astype(BF16)
        kk_ref[2 * h1 + 1, blk:, :] = k_odd.astype(BF16)
        kk_ref[2 * h1 + 0, blk:, :] = pltpu.roll(k_odd, HEAD_DIM, 1).astype(BF16)
        vt_ref[h0, :, blk:] = jnp.where(low, vt, 1.0).T.astype(BF16)
        vt_ref[h1, :, blk:] = pltpu.roll(jnp.where(low, 1.0, vt), HEAD_DIM, 1).T.astype(BF16)

    prev_keys = lax.broadcasted_iota(jnp.int32, (2 * blk, LANES), 0) < blk
    neg_prev = jnp.where(jnp.logical_and(n == 0, prev_keys), NEG, 0.0).astype(F32)
    for nb in range(rows // blk):
        r0 = nb * blk
        for t in range(ATTN_WIDTH // LANES):
            h, tt = divmod(t, 2)
            qs = qn_ref[r0:r0 + blk, t * LANES:(t + 1) * LANES]
            vals_t = vt_ref[h, :, r0:r0 + 2 * blk]
            normed = []
            for a in range(2):
                qh = 4 * h + 2 * tt + a
                keys = kk_ref[2 * h + a, r0:r0 + 2 * blk, :]
                s = lax.dot_general(keys, qs, (((1,), (1,)), ((), ())), preferred_element_type=F32)
                s = s + tbl_ref[qh]
                if nb == 0:
                    s = s + neg_prev
                sink = sink_ref[qh] * LOG2E
                m = jnp.maximum(jnp.max(s, axis=0, keepdims=True), sink)
                p = jnp.exp2(s - m).astype(BF16)
                r = jnp.dot(vals_t, p, preferred_element_type=F32)
                den = r[HEAD_DIM:HEAD_DIM + 1, :] + jnp.exp2(sink - m)
                normed.append(r[0:HEAD_DIM, :] / den)
            out_t = jnp.concatenate(normed, axis=0)
            o_ref[r0:r0 + blk, t * LANES:(t + 1) * LANES] = out_t.T.astype(o_ref.dtype)

    kk_ref[:, 0:blk, :] = kk_ref[:, rows:rows + blk, :]
    vt_ref[:, :, 0:blk] = vt_ref[:, :, rows:rows + blk]

    wub_ref[...] = wu_ref[...].astype(BF16)
    wdb_ref[...] = wd_ref[...].astype(BF16)


def _swa(proj, sinks, gq, gk, w_up, w_down, l, B, S, rows):
    T = B * S
    nsteps = S // rows
    total_steps = B * nsteps
    up_rows = D_MODEL // total_steps
    down_rows = D_FF // total_steps
    assert up_rows % 16 == 0 and up_rows * total_steps == D_MODEL and down_rows * total_steps == D_FF
    qcol = OFF_AQ // ATTN_WIDTH
    kcol = OFF_AK // KV_WIDTH
    vcol = OFF_AV // KV_WIDTH
    return pl.pallas_call(
        functools.partial(_swa_kernel, rows=rows),
        grid=(B, nsteps),
        in_specs=[
            pl.BlockSpec(memory_space=pltpu.SMEM),
            pl.BlockSpec((rows, ATTN_WIDTH), lambda b, n: (b * nsteps + n, qcol)),
            pl.BlockSpec((rows, KV_WIDTH), lambda b, n: (b * nsteps + n, kcol)),
            pl.BlockSpec((rows, KV_WIDTH), lambda b, n: (b * nsteps + n, vcol)),
            pl.BlockSpec((1, LANES), lambda b, n: (0, 0)),
            pl.BlockSpec((1, LANES), lambda b, n: (0, 0)),
            pl.BlockSpec((None, up_rows, D_FF), lambda b, n: (l, b * nsteps + n, 0)),
            pl.BlockSpec((None, down_rows, D_MODEL), lambda b, n: (l, b * nsteps + n, 0)),
        ],
        out_specs=[
            pl.BlockSpec((rows, ATTN_WIDTH), lambda b, n: (b * nsteps + n, 0)),
            pl.BlockSpec((up_rows, D_FF), lambda b, n: (b * nsteps + n, 0)),
            pl.BlockSpec((down_rows, D_MODEL), lambda b, n: (b * nsteps + n, 0)),
        ],
        out_shape=[
            jax.ShapeDtypeStruct((T, ATTN_WIDTH), BF16),
            jax.ShapeDtypeStruct((D_MODEL, D_FF), BF16),
            jax.ShapeDtypeStruct((D_FF, D_MODEL), BF16),
        ],
        scratch_shapes=[
            pltpu.VMEM((rows, ATTN_WIDTH), BF16),
            pltpu.VMEM((2 * N_KV_HEADS, rows + ATTN_BLOCK, LANES), BF16),
            pltpu.VMEM((N_KV_HEADS, LANES, rows + ATTN_BLOCK), BF16),
            pltpu.VMEM((N_Q_HEADS, 2 * ATTN_BLOCK, ATTN_BLOCK), F32),
        ],
        compiler_params=_params(2),
        name="swa",
    )(sinks, proj, proj, proj, gq, gk, w_up, w_down)


def _log_sigmoid(x):
    return jnp.minimum(x, 0.0) - jnp.log(1.0 + jnp.exp(jnp.minimum(x, -x)))


def _silu(x):
    h = 0.5 * x
    return h * (1.0 + jnp.tanh(h))


def _gla_body(q_ref, k_ref, v_refs, r_refs, z_ref, gw_ref, gb_ref, ng_ref, o_ref, s_ref, la_ref, *, rows, c,
              mxu_filler):
    pair = 2 * GLA_CHUNK

    @pl.when(c == 0)
    def _():
        s_ref[...] = jnp.zeros_like(s_ref)

    ri = lax.broadcasted_iota(jnp.int32, (pair, pair), 0)
    ci = lax.broadcasted_iota(jnp.int32, (pair, pair), 1)
    causal = jnp.logical_and(ri // GLA_CHUNK == ci // GLA_CHUNK, ci <= ri)
    tril2 = jnp.concatenate([causal, causal], axis=1).astype(BF16)
    chunk_a_rows = lax.broadcasted_iota(jnp.int32, (pair, 1), 0) < GLA_CHUNK
    b_rows_a_cols = jnp.logical_and(ri >= GLA_CHUNK, ci < GLA_CHUNK)

    logit = jnp.dot(z_ref[...].astype(BF16), gw_ref[...], preferred_element_type=F32) + gb_ref[...]
    la_ref[...] = _log_sigmoid(logit) * (1.0 / GLA_TAU)

    for pr in range(rows // pair):
        rs = slice(pr * pair, (pr + 1) * pair)
        la = la_ref[rs, :]
        la_hi = la.astype(BF16)
        la_lo = (la - la_hi.astype(F32)).astype(BF16)
        bcum = jnp.dot(tril2, jnp.concatenate([la_hi, la_lo], axis=0), preferred_element_type=F32)
        mxu_filler(1)
        q = q_ref[rs, :].astype(F32) * (GLA_DK ** -0.5)
        k = k_ref[rs, :].astype(F32)
        q_in32 = q * jnp.exp(bcum)
        q_in = q_in32.astype(BF16)
        k_in = (k * jnp.exp(-bcum)).astype(BF16)
        last_a_row = bcum[GLA_CHUNK - 1:GLA_CHUNK, :]
        q_eff = jnp.where(chunk_a_rows, q_in32, q_in32 * jnp.exp(last_a_row)).astype(BF16)
        k_cross = jnp.where(chunk_a_rows, k * jnp.exp(last_a_row - bcum), 0.0).astype(BF16)
        last_b_row = bcum[pair - 1:pair, :]
        to_pair_end = jnp.where(chunk_a_rows, last_a_row + last_b_row, last_b_row) - bcum
        k_state = (k * jnp.exp(to_pair_end)).astype(BF16)

        for h in range(GLA_HEADS):
            kc = slice(h * GLA_DK, (h + 1) * GLA_DK)
            vc = slice(h * GLA_DV, (h + 1) * GLA_DV)
            v = v_refs[h][rs, :]
            qh = q_in[:, kc]
            keys = jnp.concatenate([k_in[:, kc], k_cross[:, kc]], axis=0)
            att2 = lax.dot_general(qh, keys, (((1,), (1,)), ((), ())), preferred_element_type=F32)
            if h + 1 < GLA_HEADS:
                mxu_filler(1)
            att =jnp.where(causal, att2[:, :pair], jnp.where(b_rows_a_cols, att2[:, pair:], 0.0)).astype(BF16)
            state = s_ref[h]
            o = jnp.dot(jnp.concatenate([att, q_eff[:, kc]], axis=1),
                        jnp.concatenate([v, state.astype(BF16)], axis=0), preferred_element_type=F32)

            bt = bcum[:, kc].T
            last_ab = bt[:, GLA_CHUNK - 1:GLA_CHUNK] + bt[:, pair - 1:pair]
            s_ref[h] = state * jnp.exp(last_ab) + lax.dot_general(
                k_state[:, kc], v, (((0,), (0,)), ((), ())), preferred_element_type=F32)

            y = _rmsnorm_rows(o, ng_ref[...])
            gate = _silu(r_refs[h][rs, :].astype(F32))
            o_ref[rs, vc] = (y * gate).astype(o_ref.dtype)


def _gla_out_kernel(q_ref, k_ref, *refs, rows, nsteps):
    v_refs, r_refs = refs[:GLA_HEADS], refs[GLA_HEADS:2 * GLA_HEADS]
    (z_ref, gw_ref, gb_ref, ng_ref, x_ref, a_ref, w_ref, xo_ref,
     s_ref, la_ref, og_ref, og_prev_ref) = refs[2 * GLA_HEADS:]
    t = pl.program_id(0)

    @pl.when(t == 0)
    def _():
        og_prev_ref[...] = jnp.zeros_like(og_prev_ref)

    pieces = iter(range(2 * (D_MODEL // OUT_SLICE)))

    def mxu_filler(count):
        for piece in (next(pieces, None) for _ in range(count)):
            if piece is None:
                return
            cs = slice((piece // 2) * OUT_SLICE, (piece // 2 + 1) * OUT_SLICE)
            if piece % 2 == 0:
                xo_ref[:, cs] = x_ref[:, cs] + jnp.dot(a_ref[...], w_ref[0:ATTN_WIDTH, cs],
                                                       preferred_element_type=F32)
            else:
                xo_ref[:, cs] += jnp.dot(og_prev_ref[...], w_ref[ATTN_WIDTH:, cs],
                                         preferred_element_type=F32)

    _gla_body(q_ref, k_ref, v_refs, r_refs, z_ref, gw_ref, gb_ref, ng_ref, og_ref, s_ref, la_ref,
              rows=rows, c=lax.rem(t, nsteps), mxu_filler=mxu_filler)
    assert next(pieces, None) is None, "the GLA body must offer enough filler slots for the whole projection"
    og_prev_ref[...] = og_ref[...]


def _gla_out(proj, z, gate_w, gate_b, norm_g, x2, o_attn, w_out, l, B, S, rows):
    T = B * S
    nsteps = S // rows
    ntiles = B * nsteps
    cq = OFF_GQ // GLA_QK_WIDTH
    ck = OFF_GK // GLA_QK_WIDTH
    assert OFF_GV % GLA_DV == 0 and OFF_GR % GLA_DV == 0
    cur = lambda t: jnp.minimum(t, ntiles - 1)
    prev = lambda t: jnp.maximum(t - 1, 0)
    head_block = lambda off, h: pl.BlockSpec((rows, GLA_DV), lambda t: (cur(t), off // GLA_DV + h))
    return pl.pallas_call(
        functools.partial(_gla_out_kernel, rows=rows, nsteps=nsteps),
        grid=(ntiles + 1,),
        in_specs=[
            pl.BlockSpec((rows, GLA_QK_WIDTH), lambda t: (cur(t), cq)),
            pl.BlockSpec((rows, GLA_QK_WIDTH), lambda t: (cur(t), ck)),
            *[head_block(OFF_GV, h) for h in range(GLA_HEADS)],
            *[head_block(OFF_GR, h) for h in range(GLA_HEADS)],
            pl.BlockSpec((rows, LANES), lambda t: (cur(t), 0)),
            pl.BlockSpec((None, LANES, GLA_QK_WIDTH), lambda t: (l, 0, 0)),
            pl.BlockSpec((None, 1, GLA_QK_WIDTH), lambda t: (l, 0, 0)),
            pl.BlockSpec((None, 1, GLA_DV), lambda t: (l, 0, 0)),
            pl.BlockSpec((rows, D_MODEL), lambda t: (prev(t), 0)),
            pl.BlockSpec((rows, ATTN_WIDTH), lambda t: (prev(t), 0)),
            pl.BlockSpec((None, ATTN_WIDTH + GLA_WIDTH, D_MODEL), lambda t: (l, 0, 0)),
        ],
        out_specs=pl.BlockSpec((rows, D_MODEL), lambda t: (prev(t), 0)),
        out_shape=jax.ShapeDtypeStruct((T, D_MODEL), F32),
        scratch_shapes=[
            pltpu.VMEM((GLA_HEADS, GLA_DK, GLA_DV), F32),
            pltpu.VMEM((rows, GLA_QK_WIDTH), F32),
            pltpu.VMEM((rows, GLA_WIDTH), BF16),
            pltpu.VMEM((rows, GLA_WIDTH), BF16),
        ],
        compiler_params=_params(1),
        name="gla_out",
    )(*([proj] * (2 + 2 * GLA_HEADS)), z, gate_w, gate_b, norm_g, x2, o_attn, w_out)


def _ffn_kernel(x_ref, g_ref, wu_ref, wd_ref, o_ref, hn_ref):
    f = pl.program_id(1)

    @pl.when(f == 0)
    def _():
        x = x_ref[...]
        hn_ref[...] = _rmsnorm_rows(x, g_ref[...]).astype(BF16)
        o_ref[...] = x

    u = jnp.dot(hn_ref[...], wu_ref[...], preferred_element_type=F32)
    u = jnp.maximum(u, 0.0)
    u = (u * u).astype(BF16)
    o_ref[...] += jnp.dot(u, wd_ref[...], preferred_element_type=F32)


def _ffn(x2, g, w_up_b, w_down_b, l, tm, tf):
    T = x2.shape[0]
    return pl.pallas_call(
        _ffn_kernel,
        grid=(T // tm, D_FF // tf),
        in_specs=[
            pl.BlockSpec((tm, D_MODEL), lambda i, f: (i, 0)),
            pl.BlockSpec((None, 1, D_MODEL), lambda i, f: (l, 0, 0)),
            pl.BlockSpec((D_MODEL, tf), lambda i, f: (0, f)),
            pl.BlockSpec((tf, D_MODEL), lambda i, f: (f, 0)),
        ],
        out_specs=pl.BlockSpec((tm, D_MODEL), lambda i, f: (i, 0)),
        out_shape=jax.ShapeDtypeStruct((T, D_MODEL), F32),
        scratch_shapes=[pltpu.VMEM((tm, D_MODEL), BF16)],
        compiler_params=_params(2),
        name="ffn",
    )(x2, g, w_up_b, w_down_b)


def _tile_plan(T, S):
    tm = min(1024, T)
    swa_rows = min(1024, S)
    mix_rows = min(512, S)
    assert T % tm == 0 and S % swa_rows == 0 and S % mix_rows == 0 and mix_rows % (2 * GLA_CHUNK) == 0
    return dict(tm=tm, tn_in=2304, tf=1024, swa_rows=swa_rows, mix_rows=mix_rows)


def kernel(x, norm1_g, w_in, q_norm_g, k_norm_g, attn_sinks, gla_gate_w, gla_gate_b, gla_norm_g,
           w_out, norm2_g, w_up, w_down):
    B, S, D = x.shape
    assert D == D_MODEL
    T = B * S
    plan = _tile_plan(T, S)
    depth = w_in.shape[0]
    x2 = x.reshape(T, D)

    w_in_b = w_in.astype(BF16)
    w_z = jnp.pad(w_in[:, :, MAIN_WIDTH:], ((0, 0), (0, 0), (0, LANES - GLA_RANK))).astype(BF16)
    gate_w = jnp.pad(gla_gate_w, ((0, 0), (0, LANES - GLA_RANK), (0, 0))).astype(BF16)
    norm1 = norm1_g.reshape(depth, 1, D)
    norm2 = norm2_g.reshape(depth, 1, D)
    gate_b = gla_gate_b.reshape(depth, 1, GLA_QK_WIDTH)
    gla_g = gla_norm_g.reshape(depth, 1, GLA_DV)

    for l in range(depth):
        gq = jnp.tile(q_norm_g[l] * (HEAD_DIM ** -0.5 * LOG2E), 2).reshape(1, LANES)
        gk = jnp.tile(k_norm_g[l], 2).reshape(1, LANES)

        proj, z = _in_proj(x2, norm1, w_in_b, w_z, l, plan["tm"], plan["tn_in"])
        o_attn, w_up_b, w_down_b = _swa(proj, attn_sinks[l], gq, gk, w_up, w_down, l, B, S, plan["swa_rows"])
        x2 = _gla_out(proj, z, gate_w, gate_b, gla_g, x2, o_attn, w_out, l, B, S, plan["mix_rows"])
        x2 = _ffn(x2, norm2, w_up_b, w_down_b, l, plan["tm"], plan["tf"])
    return x2.reshape(B, S, D)
```

```python
import functools

import jax
import jax.numpy as jnp
from jax import lax
from jax.experimental import pallas as pl
from jax.experimental.pallas import tpu as pltpu

F32 = jnp.float32
BF16 = jnp.bfloat16

D_MODEL = 2048
HEAD_DIM = 64
N_Q_HEADS = 16
N_KV_HEADS = 4
ATTN_BLOCK = 128
ATTN_WIDTH = N_Q_HEADS * HEAD_DIM
KV_WIDTH = N_KV_HEADS * HEAD_DIM
GLA_HEADS = 4
GLA_DK = 128
GLA_DV = 256
GLA_QK_WIDTH = GLA_HEADS * GLA_DK
GLA_WIDTH = GLA_HEADS * GLA_DV
GLA_RANK = 16
GLA_TAU = 16.0
GLA_CHUNK = 64
D_FF = 4 * D_MODEL
EPS = 1e-6
NEG = -1e30
LOG2E = 1.4426950408889634

LANES = 128
OUT_SLICE = 256
MAIN_WIDTH = 4608
OFF_AQ, OFF_AK, OFF_AV = 0, 1024, 1280
OFF_GQ, OFF_GK, OFF_GV, OFF_GR = 1536, 2048, 2560, 3584

VMEM_LIMIT = 60 * 1024 * 1024

ALIBI_SLOPES = tuple(2.0 ** (-8.0 * (i + 1) / N_Q_HEADS) for i in range(N_Q_HEADS))


def _params(n_axes):
    return pltpu.CompilerParams(dimension_semantics=("arbitrary",) * n_axes, vmem_limit_bytes=VMEM_LIMIT)


def _rmsnorm_rows(x, g):
    ms = jnp.mean(x * x, axis=-1, keepdims=True)
    return (x * lax.rsqrt(ms + EPS)) * g


def _in_proj_kernel(x_ref, g_ref, w_ref, wz_ref, o_ref, z_ref):
    half = x_ref.shape[0] // 2
    hn0 = _rmsnorm_rows(x_ref[0:half, :], g_ref[...]).astype(BF16)
    o_ref[0:half, :] = jnp.dot(hn0, w_ref[...], preferred_element_type=F32).astype(o_ref.dtype)
    hn1 = _rmsnorm_rows(x_ref[half:, :], g_ref[...]).astype(BF16)
    o_ref[half:, :] = jnp.dot(hn1, w_ref[...], preferred_element_type=F32).astype(o_ref.dtype)
    hn = jnp.concatenate([hn0, hn1], axis=0)

    @pl.when(pl.program_id(0) == 0)
    def _():
        z_ref[...] = jnp.dot(hn, wz_ref[...], preferred_element_type=F32)


def _in_proj(x2, g, w_in_b, w_z, l, tm, tn):
    T = x2.shape[0]
    n_rows = T // tm
    z_index = lambda j, i: (jnp.where(j == 0, i, n_rows - 1), 0)
    return pl.pallas_call(
        _in_proj_kernel,
        grid=(MAIN_WIDTH // tn, n_rows),
        in_specs=[
            pl.BlockSpec((tm, D_MODEL), lambda j, i: (i, 0)),
            pl.BlockSpec((None, 1, D_MODEL), lambda j, i: (l, 0, 0)),
            pl.BlockSpec((None, D_MODEL, tn), lambda j, i: (l, 0, j)),
            pl.BlockSpec((None, D_MODEL, LANES), lambda j, i: (l, 0, 0)),
        ],
        out_specs=[
            pl.BlockSpec((tm, tn), lambda j, i: (i, j)),
            pl.BlockSpec((tm, LANES), z_index),
        ],
        out_shape=[
            jax.ShapeDtypeStruct((T, MAIN_WIDTH), BF16),
            jax.ShapeDtypeStruct((T, LANES), F32),
        ],
        compiler_params=_params(2),
        name="in_proj",
    )(x2, g, w_in_b, w_z)


def _head_meansq(xt):
    r = lax.broadcasted_iota(jnp.int32, (LANES, LANES), 0) // HEAD_DIM
    c = lax.broadcasted_iota(jnp.int32, (LANES, LANES), 1) // HEAD_DIM
    mean_blockdiag = jnp.where(r == c, 1.0 / HEAD_DIM, 0.0).astype(BF16)
    return jnp.dot((xt * xt).astype(BF16), mean_blockdiag, preferred_element_type=F32)


def _head_rmsnorm(xt, g):
    return xt * lax.rsqrt(_head_meansq(xt) + EPS) * g


def _swa_kernel(sink_ref, q_ref, k_ref, v_ref, gq_ref, gk_ref, wu_ref, wd_ref,
                o_ref, wub_ref, wdb_ref,
                qn_ref, kk_ref, vt_ref, tbl_ref, *, rows):
    b = pl.program_id(0)
    n = pl.program_id(1)
    blk = ATTN_BLOCK
    low = lax.broadcasted_iota(jnp.int32, (1, LANES), 1) < HEAD_DIM

    @pl.when(jnp.logical_and(b == 0, n == 0))
    def _():
        kj = lax.broadcasted_iota(jnp.int32, (2 * blk, blk), 0)
        qi = lax.broadcasted_iota(jnp.int32, (2 * blk, blk), 1)
        dist = qi + blk - kj
        valid = jnp.logical_and(dist >= 0, dist < blk)
        distf = dist.astype(F32)
        for qh in range(N_Q_HEADS):
            tbl_ref[qh] = jnp.where(valid, (-ALIBI_SLOPES[qh] * LOG2E) * distf, NEG)

    @pl.when(n == 0)
    def _():
        kk_ref[:, 0:blk, :] = jnp.zeros((2 * N_KV_HEADS, blk, LANES), BF16)
        vt_ref[:, :, 0:blk] = jnp.zeros((N_KV_HEADS, LANES, blk), BF16)

    for t in range(ATTN_WIDTH // LANES):
        qt = q_ref[:, t * LANES:(t + 1) * LANES].astype(F32)
        qn = _head_rmsnorm(qt, gq_ref[...])
        qn_ref[:, t * LANES:(t + 1) * LANES] = qn.astype(BF16)

    for p in range(KV_WIDTH // LANES):
        kt = _head_rmsnorm(k_ref[:, p * LANES:(p + 1) * LANES].astype(F32), gk_ref[...])
        vt = v_ref[:, p * LANES:(p + 1) * LANES].astype(F32)
        k_even = jnp.where(low, kt, 0.0)
        k_odd = jnp.where(low, 0.0, kt)
        h0, h1 = 2 * p, 2 * p + 1
        kk_ref[2 * h0 + 0, blk:, :] = k_even.astype(BF16)
        kk_ref[2 * h0 + 1, blk:, :] = pltpu.roll(k_even, HEAD_DIM, 1).astype(BF16)
        kk_ref[2 * h1 + 1, blk:, :] = k_odd.astype(BF16)
        kk_ref[2 * h1 + 0, blk:, :] = pltpu.roll(k_odd, HEAD_DIM, 1).astype(BF16)
        vt_ref[h0, :, blk:] = jnp.where(low, vt, 1.0).T.astype(BF16)
        vt_ref[h1, :, blk:] = pltpu.roll(jnp.where(low, 1.0, vt), HEAD_DIM, 1).T.astype(BF16)

    prev_keys = lax.broadcasted_iota(jnp.int32, (2 * blk, LANES), 0) < blk
    neg_prev = jnp.where(jnp.logical_and(n == 0, prev_keys), NEG, 0.0).astype(F32)
    for nb in range(rows // blk):
        r0 = nb * blk
        for t in range(ATTN_WIDTH // LANES):
            h, tt = divmod(t, 2)
            qs = qn_ref[r0:r0 + blk, t * LANES:(t + 1) * LANES]
            vals_t = vt_ref[h, :, r0:r0 + 2 * blk]
            normed = []
            for a in range(2):
                qh = 4 * h + 2 * tt + a
                keys = kk_ref[2 * h + a, r0:r0 + 2 * blk, :]
                s = lax.dot_general(keys, qs, (((1,), (1,)), ((), ())), preferred_element_type=F32)
                s = s + tbl_ref[qh]
                if nb == 0:
                    s = s + neg_prev
                sink = sink_ref[qh] * LOG2E
                m = jnp.maximum(jnp.max(s, axis=0, keepdims=True), sink)
                p = jnp.exp2(s - m).astype(BF16)
                r = jnp.dot(vals_t, p, preferred_element_type=F32)
                den = r[HEAD_DIM:HEAD_DIM + 1, :] + jnp.exp2(sink - m)
                normed.append(r[0:HEAD_DIM, :] / den)
            out_t = jnp.concatenate(normed, axis=0)
            o_ref[r0:r0 + blk, t * LANES:(t + 1) * LANES] = out_t.T.astype(o_ref.dtype)

    kk_ref[:, 0:blk, :] = kk_ref[:, rows:rows + blk, :]
    vt_ref[:, :, 0:blk] = vt_ref[:, :, rows:rows + blk]

    wub_ref[...] = wu_ref[...].astype(BF16)
    wdb_ref[...] = wd_ref[...].astype(BF16)


def _swa(proj, sinks, gq, gk, w_up, w_down, l, B, S, rows):
    T = B * S
    nsteps = S // rows
    total_steps = B * nsteps
    up_rows = D_MODEL // total_steps
    down_rows = D_FF // total_steps
    assert up_rows % 16 == 0 and up_rows * total_steps == D_MODEL and down_rows * total_steps == D_FF
    qcol = OFF_AQ // ATTN_WIDTH
    kcol = OFF_AK // KV_WIDTH
    vcol = OFF_AV // KV_WIDTH
    return pl.pallas_call(
        functools.partial(_swa_kernel, rows=rows),
        grid=(B, nsteps),
        in_specs=[
            pl.BlockSpec(memory_space=pltpu.SMEM),
            pl.BlockSpec((rows, ATTN_WIDTH), lambda b, n: (b * nsteps + n, qcol)),
            pl.BlockSpec((rows, KV_WIDTH), lambda b, n: (b * nsteps + n, kcol)),
            pl.BlockSpec((rows, KV_WIDTH), lambda b, n: (b * nsteps + n, vcol)),
            pl.BlockSpec((1, LANES), lambda b, n: (0, 0)),
            pl.BlockSpec((1, LANES), lambda b, n: (0, 0)),
            pl.BlockSpec((None, up_rows, D_FF), lambda b, n: (l, b * nsteps + n, 0)),
            pl.BlockSpec((None, down_rows, D_MODEL), lambda b, n: (l, b * nsteps + n, 0)),
        ],
        out_specs=[
            pl.BlockSpec((rows, ATTN_WIDTH), lambda b, n: (b * nsteps + n, 0)),
            pl.BlockSpec((up_rows, D_FF), lambda b, n: (b * nsteps + n, 0)),
            pl.BlockSpec((down_rows, D_MODEL), lambda b, n: (b * nsteps + n, 0)),
        ],
        out_shape=[
            jax.ShapeDtypeStruct((T, ATTN_WIDTH), BF16),
            jax.ShapeDtypeStruct((D_MODEL, D_FF), BF16),
            jax.ShapeDtypeStruct((D_FF, D_MODEL), BF16),
        ],
        scratch_shapes=[
            pltpu.VMEM((rows, ATTN_WIDTH), BF16),
            pltpu.VMEM((2 * N_KV_HEADS, rows + ATTN_BLOCK, LANES), BF16),
            pltpu.VMEM((N_KV_HEADS, LANES, rows + ATTN_BLOCK), BF16),
            pltpu.VMEM((N_Q_HEADS, 2 * ATTN_BLOCK, ATTN_BLOCK), F32),
        ],
        compiler_params=_params(2),
        name="swa",
    )(sinks, proj, proj, proj, gq, gk, w_up, w_down)


def _log_sigmoid(x):
    return jnp.minimum(x, 0.0) - jnp.log(1.0 + jnp.exp(jnp.minimum(x, -x)))


def _silu(x):
    h = 0.5 * x
    return h * (1.0 + jnp.tanh(h))


def _gla_body(q_ref, k_ref, v_refs, r_refs, z_ref, gw_ref, gb_ref, ng_ref, o_ref, s_ref, la_ref, *, rows, c,
              mxu_filler):
    pair = 2 * GLA_CHUNK

    @pl.when(c == 0)
    def _():
        s_ref[...] = jnp.zeros_like(s_ref)

    ri = lax.broadcasted_iota(jnp.int32, (pair, pair), 0)
    ci = lax.broadcasted_iota(jnp.int32, (pair, pair), 1)
    causal = jnp.logical_and(ri // GLA_CHUNK == ci // GLA_CHUNK, ci <= ri)
    tril2 = jnp.concatenate([causal, causal], axis=1).astype(BF16)
    chunk_a_rows = lax.broadcasted_iota(jnp.int32, (pair, 1), 0) < GLA_CHUNK
    b_rows_a_cols = jnp.logical_and(ri >= GLA_CHUNK, ci < GLA_CHUNK)

    logit = jnp.dot(z_ref[...].astype(BF16), gw_ref[...], preferred_element_type=F32) + gb_ref[...]
    la_ref[...] = _log_sigmoid(logit) * (1.0 / GLA_TAU)

    for pr in range(rows // pair):
        rs = slice(pr * pair, (pr + 1) * pair)
        la = la_ref[rs, :]
        la_hi = la.astype(BF16)
        la_lo = (la - la_hi.astype(F32)).astype(BF16)
        bcum = jnp.dot(tril2, jnp.concatenate([la_hi, la_lo], axis=0), preferred_element_type=F32)
        mxu_filler(1)
        q = q_ref[rs, :].astype(F32) * (GLA_DK ** -0.5)
        k = k_ref[rs, :].astype(F32)
        q_in32 = q * jnp.exp(bcum)
        q_in = q_in32.astype(BF16)
        k_in = (k * jnp.exp(-bcum)).astype(BF16)
        last_a_row = bcum[GLA_CHUNK - 1:GLA_CHUNK, :]
        q_eff = jnp.where(chunk_a_rows, q_in32, q_in32 * jnp.exp(last_a_row)).astype(BF16)
        k_cross = jnp.where(chunk_a_rows, k * jnp.exp(last_a_row - bcum), 0.0).astype(BF16)
        last_b_row = bcum[pair - 1:pair, :]
        to_pair_end = jnp.where(chunk_a_rows, last_a_row + last_b_row, last_b_row) - bcum
        k_state = (k * jnp.exp(to_pair_end)).astype(BF16)

        for h in range(GLA_HEADS):
            kc = slice(h * GLA_DK, (h + 1) * GLA_DK)
            vc = slice(h * GLA_DV, (h + 1) * GLA_DV)
            v = v_refs[h][rs, :]
            qh = q_in[:, kc]
            keys = jnp.concatenate([k_in[:, kc], k_cross[:, kc]], axis=0)
            att2 = lax.dot_general(qh, keys, (((1,), (1,)), ((), ())), preferred_element_type=F32)
            if h + 1 < GLA_HEADS:
                mxu_filler(1)
            att =jnp.where(causal, att2[:, :pair], jnp.where(b_rows_a_cols, att2[:, pair:], 0.0)).astype(BF16)
            state = s_ref[h]
            o = jnp.dot(jnp.concatenate([att, q_eff[:, kc]], axis=1),
                        jnp.concatenate([v, state.astype(BF16)], axis=0), preferred_element_type=F32)

            bt = bcum[:, kc].T
            last_ab = bt[:, GLA_CHUNK - 1:GLA_CHUNK] + bt[:, pair - 1:pair]
            s_ref[h] = state * jnp.exp(last_ab) + lax.dot_general(
                k_state[:, kc], v, (((0,), (0,)), ((), ())), preferred_element_type=F32)

            y = _rmsnorm_rows(o, ng_ref[...])
            gate = _silu(r_refs[h][rs, :].astype(F32))
            o_ref[rs, vc] = (y * gate).astype(o_ref.dtype)


def _gla_out_kernel(q_ref, k_ref, *refs, rows, nsteps):
    v_refs, r_refs = refs[:GLA_HEADS], refs[GLA_HEADS:2 * GLA_HEADS]
    (z_ref, gw_ref, gb_ref, ng_ref, x_ref, a_ref, w_ref, xo_ref,
     s_ref, la_ref, og_ref, og_prev_ref) = refs[2 * GLA_HEADS:]
    t = pl.program_id(0)

    @pl.when(t == 0)
    def _():
        og_prev_ref[...] = jnp.zeros_like(og_prev_ref)

    pieces = iter(range(2 * (D_MODEL // OUT_SLICE)))

    def mxu_filler(count):
        for piece in (next(pieces, None) for _ in range(count)):
            if piece is None:
                return
            cs = slice((piece // 2) * OUT_SLICE, (piece // 2 + 1) * OUT_SLICE)
            if piece % 2 == 0:
                xo_ref[:, cs] = x_ref[:, cs] + jnp.dot(a_ref[...], w_ref[0:ATTN_WIDTH, cs],
                                                       preferred_element_type=F32)
            else:
                xo_ref[:, cs] += jnp.dot(og_prev_ref[...], w_ref[ATTN_WIDTH:, cs],
                                         preferred_element_type=F32)

    _gla_body(q_ref, k_ref, v_refs, r_refs, z_ref, gw_ref, gb_ref, ng_ref, og_ref, s_ref, la_ref,
              rows=rows, c=lax.rem(t, nsteps), mxu_filler=mxu_filler)
    assert next(pieces, None) is None, "the GLA body must offer enough filler slots for the whole projection"
    og_prev_ref[...] = og_ref[...]


def _gla_out(proj, z, gate_w, gate_b, norm_g, x2, o_attn, w_out, l, B, S, rows):
    T = B * S
    nsteps = S // rows
    ntiles = B * nsteps
    cq = OFF_GQ // GLA_QK_WIDTH
    ck = OFF_GK // GLA_QK_WIDTH
    assert OFF_GV % GLA_DV == 0 and OFF_GR % GLA_DV == 0
    cur = lambda t: jnp.minimum(t, ntiles - 1)
    prev = lambda t: jnp.maximum(t - 1, 0)
    head_block = lambda off, h: pl.BlockSpec((rows, GLA_DV), lambda t: (cur(t), off // GLA_DV + h))
    return pl.pallas_call(
        functools.partial(_gla_out_kernel, rows=rows, nsteps=nsteps),
        grid=(ntiles + 1,),
        in_specs=[
            pl.BlockSpec((rows, GLA_QK_WIDTH), lambda t: (cur(t), cq)),
            pl.BlockSpec((rows, GLA_QK_WIDTH), lambda t: (cur(t), ck)),
            *[head_block(OFF_GV, h) for h in range(GLA_HEADS)],
            *[head_block(OFF_GR, h) for h in range(GLA_HEADS)],
            pl.BlockSpec((rows, LANES), lambda t: (cur(t), 0)),
            pl.BlockSpec((None, LANES, GLA_QK_WIDTH), lambda t: (l, 0, 0)),
            pl.BlockSpec((None, 1, GLA_QK_WIDTH), lambda t: (l, 0, 0)),
            pl.BlockSpec((None, 1, GLA_DV), lambda t: (l, 0, 0)),
            pl.BlockSpec((rows, D_MODEL), lambda t: (prev(t), 0)),
            pl.BlockSpec((rows, ATTN_WIDTH), lambda t: (prev(t), 0)),
            pl.BlockSpec((None, ATTN_WIDTH + GLA_WIDTH, D_MODEL), lambda t: (l, 0, 0)),
        ],
        out_specs=pl.BlockSpec((rows, D_MODEL), lambda t: (prev(t), 0)),
        out_shape=jax.ShapeDtypeStruct((T, D_MODEL), F32),
        scratch_shapes=[
            pltpu.VMEM((GLA_HEADS, GLA_DK, GLA_DV), F32),
            pltpu.VMEM((rows, GLA_QK_WIDTH), F32),
            pltpu.VMEM((rows, GLA_WIDTH), BF16),
            pltpu.VMEM((rows, GLA_WIDTH), BF16),
        ],
        compiler_params=_params(1),
        name="gla_out",
    )(*([proj] * (2 + 2 * GLA_HEADS)), z, gate_w, gate_b, norm_g, x2, o_attn, w_out)


def _ffn_kernel(x_ref, g_ref, wu_ref, wd_ref, o_ref, hn_ref):
    f = pl.program_id(1)

    @pl.when(f == 0)
    def _():
        x = x_ref[...]
        hn_ref[...] = _rmsnorm_rows(x, g_ref[...]).astype(BF16)
        o_ref[...] = x

    u = jnp.dot(hn_ref[...], wu_ref[...], preferred_element_type=F32)
    u = jnp.maximum(u, 0.0)
    u = (u * u).astype(BF16)
    o_ref[...] += jnp.dot(u, wd_ref[...], preferred_element_type=F32)


def _ffn(x2, g, w_up_b, w_down_b, l, tm, tf):
    T = x2.shape[0]
    return pl.pallas_call(
        _ffn_kernel,
        grid=(T // tm, D_FF // tf),
        in_specs=[
            pl.BlockSpec((tm, D_MODEL), lambda i, f: (i, 0)),
            pl.BlockSpec((None, 1, D_MODEL), lambda i, f: (l, 0, 0)),
            pl.BlockSpec((D_MODEL, tf), lambda i, f: (0, f)),
            pl.BlockSpec((tf, D_MODEL), lambda i, f: (f, 0)),
        ],
        out_specs=pl.BlockSpec((tm, D_MODEL), lambda i, f: (i, 0)),
        out_shape=jax.ShapeDtypeStruct((T, D_MODEL), F32),
        scratch_shapes=[pltpu.VMEM((tm, D_MODEL), BF16)],
        compiler_params=_params(2),
        name="ffn",
    )(x2, g, w_up_b, w_down_b)


def _tile_plan(T, S):
    tm = min(1024, T)
    swa_rows = min(1024, S)
    mix_rows = min(512, S)
    assert T % tm == 0 and S % swa_rows == 0 and S % mix_rows == 0 and mix_rows % (2 * GLA_CHUNK) == 0
    return dict(tm=tm, tn_in=2304, tf=1024, swa_rows=swa_rows, mix_rows=mix_rows)


def kernel(x, norm1_g, w_in, q_norm_g, k_norm_g, attn_sinks, gla_gate_w, gla_gate_b, gla_norm_g,
           w_out, norm2_g, w_up, w_down):
    B, S, D = x.shape
    assert D == D_MODEL
    T = B * S
    plan = _tile_plan(T, S)
    depth = w_in.shape[0]
    x2 = x.reshape(T, D)

    w_in_b = w_in.astype(BF16)
    w_z = jnp.pad(w_in[:, :, MAIN_WIDTH:], ((0, 0), (0, 0), (0, LANES - GLA_RANK))).astype(BF16)
    gate_w = jnp.pad(gla_gate_w, ((0, 0), (0, LANES - GLA_RANK), (0, 0))).astype(BF16)
    norm1 = norm1_g.reshape(depth, 1, D)
    norm2 = norm2_g.reshape(depth, 1, D)
    gate_b = gla_gate_b.reshape(depth, 1, GLA_QK_WIDTH)
    gla_g = gla_norm_g.reshape(depth, 1, GLA_DV)

    for l in range(depth):
        gq = jnp.tile(q_norm_g[l] * (HEAD_DIM ** -0.5 * LOG2E), 2).reshape(1, LANES)
        gk = jnp.tile(k_norm_g[l], 2).reshape(1, LANES)

        proj, z = _in_proj(x2, norm1, w_in_b, w_z, l, plan["tm"], plan["tn_in"])
        o_attn, w_up_b, w_down_b = _swa(proj, attn_sinks[l], gq, gk, w_up, w_down, l, B, S, plan["swa_rows"])
        x2 = _gla_out(proj, z, gate_w, gate_b, gla_g, x2, o_attn, w_out, l, B, S, plan["mix_rows"])
        x2 = _ffn(x2, norm2, w_up_b, w_down_b, l, plan["tm"], plan["tf"])
    return x2.reshape(B, S, D)
```
